```python
import jax, jax.numpy as jnp
from jax import lax
import numpy as np

D_MODEL = 1024
BATCH = 8
SEQ = 2048
DEPTH = 2
DEC_BATCH = 32
DEC_SEQ = 4
PAST_LEN = 8192
PAGE_SIZE = 128

N_MIXERS = 2
N_POOL_LAYERS = (DEPTH + 1) // 2
N_ATTN_LAYERS = DEPTH // 2
POOL_EXPAND = 2
POOL_WIDTH = POOL_EXPAND * D_MODEL
POOL_WINDOWS = (2, 4, 8, 16)
N_POOL_GROUPS = len(POOL_WINDOWS)
POOL_GROUP = POOL_WIDTH // N_POOL_GROUPS
POOL_CTX = max(POOL_WINDOWS) - 1
HEAD_DIM = 64
N_HEADS = D_MODEL // HEAD_DIM
ATTN_WIDTH = N_HEADS * HEAD_DIM
Q_BLOCK = 128
FORGET_BIAS_MIN = 1.0
FORGET_BIAS_MAX = 10.0
RMS_EPS = 1e-6
F32 = jnp.float32

kernel_name = "pool_fox_hybrid_step"


def rms_norm(x, g):
    xf = x.astype(F32)
    y = xf * lax.rsqrt(jnp.mean(xf * xf, axis=-1, keepdims=True) + RMS_EPS)
    return (y * g.astype(F32)).astype(x.dtype)


def pool_mixer(h, prefix, start, w_in, mix, scale, w_out):
    T = h.shape[1]
    uz = h @ w_in
    u, z = jnp.split(uz, 2, axis=-1)
    ext = jnp.concatenate([prefix.astype(u.dtype), u], axis=1)
    cs = jnp.cumsum(ext.astype(F32), axis=1)
    cs = jnp.pad(cs, ((0, 0), (1, 0), (0, 0)))
    pos = start + jnp.arange(T, dtype=jnp.int32)
    u32 = u.astype(F32)
    outs = []
    for g, w in enumerate(POOL_WINDOWS):
        sl = slice(g * POOL_GROUP, (g + 1) * POOL_GROUP)
        hi = cs[:, POOL_CTX + 1:POOL_CTX + 1 + T, sl]
        lo = cs[:, POOL_CTX + 1 - w:POOL_CTX + 1 - w + T, sl]
        cnt = jnp.minimum(pos + 1, w).astype(F32)[None, :, None]
        pooled = (hi - lo) / cnt - u32[..., sl]
        outs.append(jnp.einsum('btc,cd->btd', pooled, mix[g].astype(F32)))
    y = jnp.concatenate(outs, axis=-1) * scale.astype(F32)
    y = (y * jax.nn.silu(z.astype(F32))).astype(h.dtype)
    return y @ w_out, ext[:, -POOL_CTX:]


def fox_project(h, w_in, b_f):
    B, T, _ = h.shape
    proj = h @ w_in
    q, k, v, z, f = jnp.split(proj, [ATTN_WIDTH, 2 * ATTN_WIDTH, 3 * ATTN_WIDTH, 4 * ATTN_WIDTH], axis=-1)
    shp = (B, T, N_HEADS, HEAD_DIM)
    logf = jax.nn.log_sigmoid(f.astype(F32) + b_f.astype(F32))
    return q.reshape(shp), k.reshape(shp), v.reshape(shp), z, logf


def fox_attend(q, cq, qpos, k, ck, kpos, v):
    B, T = q.shape[:2]
    blk = Q_BLOCK if T % Q_BLOCK == 0 else T
    nb = T // blk
    scale = HEAD_DIM ** -0.5
    ckT = jnp.swapaxes(ck, 1, 2)

    def one(args):
        qb, cqb, qpb = args
        s = jnp.einsum('bqhd,bkhd->bhqk', qb, k, preferred_element_type=F32) * scale
        s = s + jnp.swapaxes(cqb, 1, 2)[..., :, None] - ckT[:, :, None, :]
        s = jnp.where((kpos[None, :] <= qpb[:, None])[None, None], s, -jnp.inf)
        p = jax.nn.softmax(s, axis=-1)
        return jnp.einsum('bhqk,bkhd->bqhd', p.astype(v.dtype), v)

    qs = q.reshape(B, nb, blk, N_HEADS, HEAD_DIM).transpose(1, 0, 2, 3, 4)
    cqs = cq.reshape(B, nb, blk, N_HEADS).transpose(1, 0, 2, 3)
    qps = qpos.reshape(nb, blk)
    o = lax.map(one, (qs, cqs, qps))
    return o.transpose(1, 0, 2, 3, 4).reshape(B, T, ATTN_WIDTH)


def fox_prompt(h, w_in, b_f, w_out):
    T = h.shape[1]
    q, k, v, z, logf = fox_project(h, w_in, b_f)
    c = jnp.cumsum(logf, axis=1)
    pos = jnp.arange(T, dtype=jnp.int32)
    o = fox_attend(q, c, pos, k, c, pos, v)
    y = (o * jax.nn.silu(z)).astype(h.dtype) @ w_out
    return y, k, v, logf


def fox_sample(h, ck_pool, cv_pool, cl_pool, page_table, w_in, b_f, w_out):
    B, T, _ = h.shape
    P = page_table.shape[1] * ck_pool.shape[1]
    q, k, v, z, logf = fox_project(h, w_in, b_f)
    kp = ck_pool[page_table].reshape(B, P, N_HEADS, HEAD_DIM).astype(k.dtype)
    vp = cv_pool[page_table].reshape(B, P, N_HEADS, HEAD_DIM).astype(v.dtype)
    lp = cl_pool[page_table].reshape(B, P, N_HEADS).astype(F32)
    c_past = lp - lax.cumsum(lp, axis=1, reverse=True)
    c_new = jnp.cumsum(logf, axis=1)
    k_all = jnp.concatenate([kp, k], axis=1)
    v_all = jnp.concatenate([vp, v], axis=1)
    c_all = jnp.concatenate([c_past, c_new], axis=1)
    kpos = jnp.arange(P + T, dtype=jnp.int32)
    qpos = P + jnp.arange(T, dtype=jnp.int32)
    o = fox_attend(q, c_new, qpos, k_all, c_all, kpos, v_all)
    y = (o * jax.nn.silu(z)).astype(h.dtype) @ w_out
    return y, k, v, logf


def setup_inputs(seed: int = 0) -> dict:
    key = jax.random.key(seed)
    ks = jax.random.split(key, 20)
    nrm = jax.random.normal
    n_pages = PAST_LEN // PAGE_SIZE
    n_used = DEC_BATCH * n_pages
    n_phys = n_used + max(1, n_used // 4)
    page_table = jax.random.permutation(ks[0], n_phys)[:n_used].reshape(DEC_BATCH, n_pages).astype(jnp.int32)
    head_bias = jnp.linspace(FORGET_BIAS_MIN, FORGET_BIAS_MAX, N_HEADS, dtype=F32)
    return {
        "x_prompt": nrm(ks[1], (BATCH, SEQ, D_MODEL), F32),
        "x_sample": nrm(ks[2], (DEC_BATCH, DEC_SEQ, D_MODEL), F32),
        "state_pool": nrm(ks[3], (N_POOL_LAYERS, DEC_BATCH, POOL_CTX, POOL_WIDTH), F32),
        "cache_k": nrm(ks[4], (N_ATTN_LAYERS, n_phys, PAGE_SIZE, N_HEADS, HEAD_DIM), F32),
        "cache_v": nrm(ks[5], (N_ATTN_LAYERS, n_phys, PAGE_SIZE, N_HEADS, HEAD_DIM), F32),
        "cache_logf": jax.nn.log_sigmoid(head_bias + 0.5 * nrm(ks[6], (N_ATTN_LAYERS, n_phys, PAGE_SIZE, N_HEADS), F32)),
        "page_table": page_table,
        "norm_g": 1.0 + 0.02 * nrm(ks[7], (DEPTH, D_MODEL), F32),
        "w_in_pool": nrm(ks[8], (N_POOL_LAYERS, D_MODEL, 2 * POOL_WIDTH), F32) * D_MODEL ** -0.5,
        "pool_mix": nrm(ks[9], (N_POOL_LAYERS, N_POOL_GROUPS, POOL_GROUP, POOL_GROUP), F32) * POOL_GROUP ** -0.5,
        "pool_scale": 1.0 + 0.1 * nrm(ks[10], (N_POOL_LAYERS, POOL_WIDTH), F32),
        "w_out_pool": nrm(ks[11], (N_POOL_LAYERS, POOL_WIDTH, D_MODEL), F32) * POOL_WIDTH ** -0.5,
        "w_in_attn": nrm(ks[12], (N_ATTN_LAYERS, D_MODEL, 4 * ATTN_WIDTH + N_HEADS), F32) * D_MODEL ** -0.5,
        "b_forget": head_bias + 0.1 * nrm(ks[13], (N_ATTN_LAYERS, N_HEADS), F32),
        "w_out_attn": nrm(ks[14], (N_ATTN_LAYERS, ATTN_WIDTH, D_MODEL), F32) * ATTN_WIDTH ** -0.5,
        "final_norm": 1.0 + 0.02 * nrm(ks[15], (D_MODEL,), F32),
    }


def reference(x_prompt, x_sample, state_pool, cache_k, cache_v, cache_logf, page_table,
              norm_g, w_in_pool, pool_mix, pool_scale, w_out_pool,
              w_in_attn, b_forget, w_out_attn, final_norm):
    P = page_table.shape[1] * cache_k.shape[2]
    xp, xs = x_prompt, x_sample
    pool_p, pool_s = [], []
    kp_l, vp_l, lp_l, ks_l, vs_l, ls_l = [], [], [], [], [], []
    for i in range(DEPTH):
        j = i // N_MIXERS
        hp = rms_norm(xp, norm_g[i])
        hs = rms_norm(xs, norm_g[i])
        if i % N_MIXERS == 0:
            zero_prefix = jnp.zeros((hp.shape[0], POOL_CTX, POOL_WIDTH), hp.dtype)
            yp, sp = pool_mixer(hp, zero_prefix, 0, w_in_pool[j], pool_mix[j], pool_scale[j], w_out_pool[j])
            ys, ss = pool_mixer(hs, state_pool[j], P, w_in_pool[j], pool_mix[j], pool_scale[j], w_out_pool[j])
            pool_p.append(sp)
            pool_s.append(ss)
        else:
            yp, kp, vp, lp = fox_prompt(hp, w_in_attn[j], b_forget[j], w_out_attn[j])
            ys, kn, vn, ln = fox_sample(hs, cache_k[j], cache_v[j], cache_logf[j], page_table,
                                        w_in_attn[j], b_forget[j], w_out_attn[j])
            kp_l.append(kp); vp_l.append(vp); lp_l.append(lp)
            ks_l.append(kn); vs_l.append(vn); ls_l.append(ln)
        xp = xp + yp
        xs = xs + ys
    y_prompt = rms_norm(xp, final_norm)
    y_sample = rms_norm(xs, final_norm)
    return (y_prompt, y_sample, jnp.stack(pool_p), jnp.stack(pool_s),
            jnp.stack(kp_l), jnp.stack(vp_l), jnp.stack(lp_l),
            jnp.stack(ks_l), jnp.stack(vs_l), jnp.stack(ls_l))
```

```python
import functools

import jax
import jax.numpy as jnp
from jax import lax
from jax.experimental import pallas as pl
from jax.experimental.pallas import tpu as pltpu

F32 = jnp.float32
BF16 = jnp.bfloat16

D_MODEL = 1024
POOL_WIDTH = 2048
POOL_WINDOWS = (2, 4, 8, 16)
POOL_GROUP = POOL_WIDTH // len(POOL_WINDOWS)
POOL_CTX = max(POOL_WINDOWS) - 1
POOL_HALO = 16
N_HEADS = 16
HEAD_DIM = 64
HEAD_PAIRS = N_HEADS // 2
PAIR_WIDTH = 2 * HEAD_DIM
RMS_EPS = 1e-6
NEG_BIG = -1e30

ROW_TILE = 256
Q_TILE = 512
KV_TILE = 512
PAGE = 128
PAGES_PER_STEP = 8

VMEM_LIMIT = 56 * 1024 * 1024


def _rms_norm(x, g):
    ms = jnp.mean(x * x, axis=-1, keepdims=True)
    return x * lax.rsqrt(ms + RMS_EPS) * g


def _silu(z):
    return z * jax.nn.sigmoid(z)


def _log_sigmoid(x):
    return jnp.minimum(x, 0.0) - jnp.log1p(jnp.exp(-jnp.abs(x)))


def _const_spec(shape):
    return pl.BlockSpec(shape, lambda *_: (0,) * len(shape), pipeline_mode=pl.Buffered(1))


def _pool_tail(hb, pooled_fn, win_ref, mix_ref, scale_ref, wout_ref, resid):
    acc = resid
    for g in range(len(POOL_WINDOWS)):
        cols = slice(g * POOL_GROUP, (g + 1) * POOL_GROUP)
        zcols = slice(POOL_WIDTH + g * POOL_GROUP, POOL_WIDTH + (g + 1) * POOL_GROUP)
        u = jnp.dot(hb, win_ref[:, cols], preferred_element_type=F32)
        pooled = pooled_fn(g, u)
        y = jnp.dot(pooled.astype(BF16), mix_ref[g], preferred_element_type=F32)
        y = y * scale_ref[:, cols]
        z = jnp.dot(hb, win_ref[:, zcols], preferred_element_type=F32)
        yg = (y * _silu(z)).astype(BF16)
        acc = acc + jnp.dot(yg, wout_ref[cols, :], preferred_element_type=F32)
    return acc


def _pool_prompt_body(x_ref, g_ref, win_ref, mix_ref, scale_ref, wout_ref,
                      x1_ref, last_ref, halo_ref):
    t = pl.program_id(1)

    @pl.when(t == 0)
    def _():
        halo_ref[...] = jnp.zeros_like(halo_ref)

    x = x_ref[0]
    hb = _rms_norm(x, g_ref[...]).astype(BF16)
    pos = t * ROW_TILE + lax.broadcasted_iota(jnp.int32, (ROW_TILE, 1), 0)

    def pooled_fn(g, u):
        w = POOL_WINDOWS[g]
        cols = slice(g * POOL_GROUP, (g + 1) * POOL_GROUP)
        e = jnp.concatenate([halo_ref[:, cols], u], axis=0)
        tail = u[ROW_TILE - POOL_HALO:, :]
        halo_ref[:, cols] = tail
        last_ref[0, :, cols] = tail
        sh = 1
        while sh < w:
            e = e + pltpu.roll(e, sh, 0)
            sh *= 2
        inv = 1.0 / jnp.minimum(pos + 1, w).astype(F32)
        return e[POOL_HALO:, :] * inv - u

    x1_ref[0] = _pool_tail(hb, pooled_fn, win_ref, mix_ref, scale_ref, wout_ref, x)


def _pool_prompt(x, g, win, mix, scale, wout):
    b, t, d = x.shape
    grid = (b, t // ROW_TILE)
    return pl.pallas_call(
        _pool_prompt_body,
        grid=grid,
        in_specs=[
            pl.BlockSpec((1, ROW_TILE, d), lambda i, j: (i, j, 0)),
            _const_spec((1, d)),
            _const_spec(win.shape),
            _const_spec(mix.shape),
            _const_spec((1, POOL_WIDTH)),
            _const_spec(wout.shape),
        ],
        out_specs=[
            pl.BlockSpec((1, ROW_TILE, d), lambda i, j: (i, j, 0)),
            pl.BlockSpec((1, POOL_HALO, POOL_WIDTH), lambda i, j: (i, 0, 0)),
        ],
        out_shape=[
            jax.ShapeDtypeStruct((b, t, d), F32),
            jax.ShapeDtypeStruct((b, POOL_HALO, POOL_WIDTH), F32),
        ],
        scratch_shapes=[pltpu.VMEM((POOL_HALO, POOL_WIDTH), F32)],
        compiler_params=pltpu.CompilerParams(
            dimension_semantics=("parallel", "arbitrary"),
            vmem_limit_bytes=VMEM_LIMIT),
        name="pool_prompt",
    )(x, g, win, mix, scale, wout)


def _pool_sample_body(n_seq, n_new, start, x_ref, st_ref, g_ref, win_ref, mix_ref,
                      scale_ref, wout_ref, x1_ref, u_ref):
    x = x_ref[...]
    hb = _rms_norm(x, g_ref[...]).astype(BF16)

    def pooled_fn(g, u):
        w = POOL_WINDOWS[g]
        cols = slice(g * POOL_GROUP, (g + 1) * POOL_GROUP)
        u_ref[:, cols] = u

        def slab(e):
            if e < POOL_CTX:
                return st_ref[e * n_seq:(e + 1) * n_seq, cols]
            return u[(e - POOL_CTX) * n_seq:(e - POOL_CTX + 1) * n_seq, :]

        outs = []
        for t in range(n_new):
            s = slab(POOL_CTX + t)
            for e in range(POOL_CTX + t - w + 1, POOL_CTX + t):
                s = s + slab(e)
            cnt = min(start + t + 1, w)
            outs.append(s * (1.0 / cnt) - slab(POOL_CTX + t))
        return jnp.concatenate(outs, axis=0)

    x1_ref[...] = _pool_tail(hb, pooled_fn, win_ref, mix_ref, scale_ref, wout_ref, x)


def _pool_sample(x_tm, state_tm, n_seq, n_new, start, g, win, mix, scale, wout):
    rows, d = x_tm.shape
    return pl.pallas_call(
        functools.partial(_pool_sample_body, n_seq, n_new, start),
        out_shape=[
            jax.ShapeDtypeStruct((rows, d), F32),
            jax.ShapeDtypeStruct((rows, POOL_WIDTH), F32),
        ],
        compiler_params=pltpu.CompilerParams(vmem_limit_bytes=VMEM_LIMIT),
        name="pool_sample",
    )(x_tm, state_tm, g, win, mix, scale, wout)


def _project(x, g_ref, w_ref, wf_ref, bf_ref):
    hb = _rms_norm(x, g_ref[...]).astype(BF16)
    parts = [jnp.dot(hb, w_ref[:, i * D_MODEL:(i + 1) * D_MODEL], preferred_element_type=F32)
             for i in range(4)]
    f = jnp.dot(hb, wf_ref[...], preferred_element_type=F32) + bf_ref[...]
    q = parts[0] * (HEAD_DIM ** -0.5)
    return q, parts[1], parts[2], _silu(parts[3]), _log_sigmoid(f)


def _lane_cumsum(x):
    lane = lax.broadcasted_iota(jnp.int32, x.shape, 1)
    sh = 1
    while sh < x.shape[1]:
        x = x + jnp.where(lane >= sh, pltpu.roll(x, sh, 1), 0.0)
        sh *= 2
    return x


def _proj_prompt_body(x_ref, g_ref, w_ref, wf_ref, bf_ref,
                      q_ref, kp_ref, vp_ref, k_ref, v_ref, gate_ref, logf_ref, bias_ref,
                      carry_ref):
    t = pl.program_id(1)

    @pl.when(t == 0)
    def _():
        carry_ref[...] = jnp.zeros_like(carry_ref)

    q, k, v, gate, logf = _project(x_ref[0], g_ref, w_ref, wf_ref, bf_ref)
    k_ref[0] = k
    v_ref[0] = v
    gate_ref[0] = gate.astype(BF16)
    logf_ref[0] = logf[:, :N_HEADS]
    qb, kb, vb = q.astype(BF16), k.astype(BF16), v.astype(BF16)
    for p in range(HEAD_PAIRS):
        cols = slice(p * PAIR_WIDTH, (p + 1) * PAIR_WIDTH)
        q_ref[0, p] = qb[:, cols]
        kp_ref[0, p] = kb[:, cols]
        vp_ref[0, p] = vb[:, cols]

    lt = logf.T[:N_HEADS, :]
    carry = carry_ref[...]
    for c in range(ROW_TILE // 128):
        csum = _lane_cumsum(lt[:, c * 128:(c + 1) * 128]) + carry
        neg = -csum
        for p in range(HEAD_PAIRS):
            bias_ref[0, p, 0, :, c * 128:(c + 1) * 128] = neg[2 * p:2 * p + 2, :]
        carry = jnp.broadcast_to(csum[:, 127:128], carry.shape)
    carry_ref[...] = carry


def _proj_prompt(x1, g, w, wf, bf):
    b, t, d = x1.shape
    grid = (b, t // ROW_TILE)
    per_kv = KV_TILE // ROW_TILE
    row_spec = pl.BlockSpec((1, ROW_TILE, d), lambda i, j: (i, j, 0))
    pair_spec = pl.BlockSpec((1, HEAD_PAIRS, ROW_TILE, PAIR_WIDTH), lambda i, j: (i, 0, j, 0))
    pair_shape = jax.ShapeDtypeStruct((b, HEAD_PAIRS, t, PAIR_WIDTH), BF16)
    return pl.pallas_call(
        _proj_prompt_body,
        grid=grid,
        in_specs=[row_spec, _const_spec((1, d)), _const_spec(w.shape),
                  _const_spec(wf.shape), _const_spec(bf.shape)],
        out_specs=[
            pair_spec, pair_spec, pair_spec, row_spec, row_spec, row_spec,
            pl.BlockSpec((1, ROW_TILE, N_HEADS), lambda i, j: (i, j, 0)),
            pl.BlockSpec((1, HEAD_PAIRS, 1, 2, ROW_TILE),
                         lambda i, j: (i, 0, j // per_kv, 0, j % per_kv)),
        ],
        out_shape=[
            pair_shape, pair_shape, pair_shape,
            jax.ShapeDtypeStruct((b, t, d), F32),
            jax.ShapeDtypeStruct((b, t, d), F32),
            jax.ShapeDtypeStruct((b, t, d), BF16),
            jax.ShapeDtypeStruct((b, t, N_HEADS), F32),
            jax.ShapeDtypeStruct((b, HEAD_PAIRS, t // KV_TILE, 2, KV_TILE), F32),
        ],
        scratch_shapes=[pltpu.VMEM((N_HEADS, 128), F32)],
        compiler_params=pltpu.CompilerParams(
            dimension_semantics=("parallel", "arbitrary"),
            vmem_limit_bytes=VMEM_LIMIT),
        name="proj_prompt",
    )(x1, g, w, wf, bf)


def _proj_sample_body(x_ref, g_ref, w_ref, wf_ref, bf_ref,
                      q_ref, k_ref, v_ref, gate_ref, logf_ref):
    q, k, v, gate, logf = _project(x_ref[...], g_ref, w_ref, wf_ref, bf_ref)
    q_ref[...] = q.astype(BF16)
    k_ref[...] = k
    v_ref[...] = v
    gate_ref[...] = gate.astype(BF16)
    logf_ref[...] = logf[:, :N_HEADS]


def _proj_sample(x1, g, w, wf, bf):
    rows, d = x1.shape
    return pl.pallas_call(
        _proj_sample_body,
        out_shape=[
            jax.ShapeDtypeStruct((rows, d), BF16),
            jax.ShapeDtypeStruct((rows, d), F32),
            jax.ShapeDtypeStruct((rows, d), F32),
            jax.ShapeDtypeStruct((rows, d), BF16),
            jax.ShapeDtypeStruct((rows, N_HEADS), F32),
        ],
        compiler_params=pltpu.CompilerParams(vmem_limit_bytes=VMEM_LIMIT),
        name="proj_sample",
    )(x1, g, w, wf, bf)


def _attn_prompt_body(q_ref, k_ref, v_ref, bias_ref, o_ref):
    qi = pl.program_id(2)
    q = q_ref[0, 0]
    lane = lax.broadcasted_iota(jnp.int32, (1, PAIR_WIDTH), 1)
    first = lane < HEAD_DIM
    zero = jnp.zeros_like(q)
    q_heads = (jnp.where(first, q, zero), jnp.where(first, zero, q))
    row = lax.broadcasted_iota(jnp.int32, (Q_TILE, KV_TILE), 0)
    col = lax.broadcasted_iota(jnp.int32, (Q_TILE, KV_TILE), 1)
    visible = row >= col

    def tile(j, carry, diagonal):
        m0, l0, m1, l1, acc = carry
        start = pl.multiple_of(j * KV_TILE, KV_TILE)
        kt = k_ref[0, 0, pl.ds(start, KV_TILE), :]
        vt = v_ref[0, 0, pl.ds(start, KV_TILE), :]
        bt = bias_ref[0, 0, j]
        new = []
        for hh, (m, l) in enumerate(((m0, l0), (m1, l1))):
            s = lax.dot_general(q_heads[hh], kt, (((1,), (1,)), ((), ())),
                                preferred_element_type=F32)
            s = s + bt[hh:hh + 1, :]
            if diagonal:
                s = jnp.where(visible, s, NEG_BIG)
            m_new = jnp.maximum(m, jnp.max(s, axis=1, keepdims=True))
            alpha = jnp.exp(m - m_new)
            p = jnp.exp(s - m_new)
            l_new = alpha * l + jnp.sum(p, axis=1, keepdims=True)
            pv = jnp.dot(p.astype(BF16), vt, preferred_element_type=F32)
            new.append((m_new, l_new, alpha, pv))
        alpha = jnp.where(first, new[0][2], new[1][2])
        pv = jnp.where(first, new[0][3], new[1][3])
        return new[0][0], new[0][1], new[1][0], new[1][1], acc * alpha + pv

    neg = jnp.full((Q_TILE, 1), NEG_BIG, F32)
    zcol = jnp.zeros((Q_TILE, 1), F32)
    init = (neg, zcol, neg, zcol, jnp.zeros((Q_TILE, PAIR_WIDTH), F32))
    carry = lax.fori_loop(0, qi, lambda j, c: tile(j, c, False), init)
    _, l0, _, l1, acc = tile(qi, carry, True)
    o_ref[0] = (acc * jnp.where(first, 1.0 / l0, 1.0 / l1)).astype(BF16)


def _attn_prompt(q, k, v, bias):
    b, hp, t, w = q.shape
    grid = (b, hp, t // Q_TILE)
    return pl.pallas_call(
        _attn_prompt_body,
        grid=grid,
        in_specs=[
            pl.BlockSpec((1, 1, Q_TILE, w), lambda i, p, j: (i, p, j, 0)),
            pl.BlockSpec((1, 1, t, w), lambda i, p, j: (i, p, 0, 0)),
            pl.BlockSpec((1, 1, t, w), lambda i, p, j: (i, p, 0, 0)),
            pl.BlockSpec((1, 1, t // KV_TILE, 2, KV_TILE), lambda i, p, j: (i, p, 0, 0, 0)),
        ],
        out_specs=pl.BlockSpec((1, Q_TILE, w), lambda i, p, j: (i, j, p)),
        out_shape=jax.ShapeDtypeStruct((b, t, hp * w), BF16),
        compiler_params=pltpu.CompilerParams(
            dimension_semantics=("parallel", "parallel", "arbitrary"),
            vmem_limit_bytes=VMEM_LIMIT),
        name="attn_prompt",
    )(q, k, v, bias)


def _attn_sample_body(n_new, groups, pt_ref, q_ref, kn_ref, vn_ref, ln_ref,
                      ck_hbm, cv_hbm, cl_hbm, o_ref,
                      kbuf, vbuf, lbuf, sems, qbd_ref, kpad, vpad, lpad,
                      m_ref, l_ref, acc_ref, carry_ref):
    b = pl.program_id(0)
    gi = pl.program_id(1)
    n_b = pl.num_programs(0)
    step = b * groups + gi
    slot = step % 2
    rows = n_new * N_HEADS
    keys = PAGES_PER_STEP * PAGE

    def copies(bb, gg, sl):
        out = []
        for p in range(PAGES_PER_STEP):
            page = pt_ref[bb, (groups - 1 - gg) * PAGES_PER_STEP + p]
            rows_p = pl.ds(p * PAGE, PAGE)
            out.append(pltpu.make_async_copy(ck_hbm.at[page], kbuf.at[sl, rows_p], sems.at[0, sl]))
            out.append(pltpu.make_async_copy(cv_hbm.at[page], vbuf.at[sl, rows_p], sems.at[1, sl]))
            out.append(pltpu.make_async_copy(cl_hbm.at[page], lbuf.at[sl, p], sems.at[2, sl]))
        return out

    @pl.when(step == 0)
    def _():
        for c in copies(0, 0, 0):
            c.start()
        kpad[...] = jnp.zeros_like(kpad)
        vpad[...] = jnp.zeros_like(vpad)
        lpad[...] = jnp.zeros_like(lpad)

    @pl.when(step + 1 < n_b * groups)
    def _():
        nxt = step + 1
        for c in copies(nxt // groups, nxt % groups, 1 - slot):
            c.start()

    row_head = lax.broadcasted_iota(jnp.int32, (N_HEADS, D_MODEL), 0)
    lane_head = lax.broadcasted_iota(jnp.int32, (N_HEADS, D_MODEL), 1) // HEAD_DIM
    own = row_head == lane_head

    def suffix_bias(lp_t, carry):
        incl = _lane_cumsum(lp_t)
        total = jnp.broadcast_to(incl[:, 127:128], incl.shape)
        return total - incl + carry, carry + total

    def transpose_page(x):
        eye = (lax.broadcasted_iota(jnp.int32, (N_HEADS, N_HEADS), 0) ==
               lax.broadcasted_iota(jnp.int32, (N_HEADS, N_HEADS), 1)).astype(F32)
        return lax.dot_general(eye, x, (((1,), (1,)), ((), ())),
                               precision=lax.Precision.HIGHEST, preferred_element_type=F32)

    def accumulate(s, vb):
        m_old = m_ref[...]
        m_new = jnp.maximum(m_old, jnp.max(s, axis=1, keepdims=True))
        alpha = jnp.exp(m_old - m_new)
        p = jnp.exp(s - m_new)
        l_ref[...] = alpha * l_ref[...] + jnp.sum(p, axis=1, keepdims=True)
        acc_ref[...] = acc_ref[...] * alpha + jnp.dot(p.astype(BF16), vb,
                                                      preferred_element_type=F32)
        m_ref[...] = m_new

    def scores(kb):
        return lax.dot_general(qbd_ref[...], kb, (((1,), (1,)), ((), ())),
                               preferred_element_type=F32)

    @pl.when(gi == 0)
    def _():
        q = q_ref[0].astype(F32)
        for t in range(n_new):
            qt = jnp.broadcast_to(q[t:t + 1, :], (N_HEADS, D_MODEL))
            qbd_ref[t * N_HEADS:(t + 1) * N_HEADS, :] = jnp.where(own, qt, 0.0).astype(BF16)
        m_ref[...] = jnp.full_like(m_ref, NEG_BIG)
        l_ref[...] = jnp.zeros_like(l_ref)
        acc_ref[...] = jnp.zeros_like(acc_ref)
        kpad[0:n_new, :] = kn_ref[0]
        vpad[0:n_new, :] = vn_ref[0]
        lpad[0:n_new, :] = ln_ref[0]
        bias, carry = suffix_bias(transpose_page(lpad[...]), jnp.zeros((N_HEADS, 128), F32))
        carry_ref[...] = carry
        s = scores(kpad[...].astype(BF16)) + jnp.concatenate([bias] * n_new, axis=0)
        q_idx = lax.broadcasted_iota(jnp.int32, (rows, PAGE), 0) // N_HEADS
        k_idx = lax.broadcasted_iota(jnp.int32, (rows, PAGE), 1)
        s = jnp.where(k_idx <= q_idx, s, NEG_BIG)
        accumulate(s, vpad[...].astype(BF16))

    for c in copies(b, gi, slot):
        c.wait()

    carry = carry_ref[...]
    biases = [None] * PAGES_PER_STEP
    for p in reversed(range(PAGES_PER_STEP)):
        biases[p], carry = suffix_bias(transpose_page(lbuf[slot, p]), carry)
    carry_ref[...] = carry
    bias = jnp.concatenate(biases, axis=1)
    s = scores(kbuf[slot].astype(BF16)) + jnp.concatenate([bias] * n_new, axis=0)
    accumulate(s, vbuf[slot].astype(BF16))

    @pl.when(gi == groups - 1)
    def _():
        acc = acc_ref[...] * (1.0 / l_ref[...])
        for t in range(n_new):
            blk = jnp.where(own, acc[t * N_HEADS:(t + 1) * N_HEADS, :], 0.0)
            o_ref[0, t:t + 1, :] = jnp.sum(blk, axis=0, keepdims=True)


def _attn_sample(page_table, q, k_new, v_new, logf_new, cache_k, cache_v, cache_logf):
    n_seq, n_new, d = q.shape
    n_pages = page_table.shape[1]
    groups = n_pages // PAGES_PER_STEP
    rows = n_new * N_HEADS
    keys = PAGES_PER_STEP * PAGE
    new_spec = pl.BlockSpec((1, n_new, d), lambda i, j, pt: (i, 0, 0))
    any_spec = pl.BlockSpec(memory_space=pl.ANY)
    grid_spec = pltpu.PrefetchScalarGridSpec(
        num_scalar_prefetch=1,
        grid=(n_seq, groups),
        in_specs=[new_spec, new_spec, new_spec,
                  pl.BlockSpec((1, n_new, N_HEADS), lambda i, j, pt: (i, 0, 0)),
                  any_spec, any_spec, any_spec],
        out_specs=new_spec,
        scratch_shapes=[
            pltpu.VMEM((2, keys, d), F32),
            pltpu.VMEM((2, keys, d), F32),
            pltpu.VMEM((2, PAGES_PER_STEP, PAGE, N_HEADS), F32),
            pltpu.SemaphoreType.DMA((3, 2)),
            pltpu.VMEM((rows, d), BF16),
            pltpu.VMEM((PAGE, d), F32),
            pltpu.VMEM((PAGE, d), F32),
            pltpu.VMEM((PAGE, N_HEADS), F32),
            pltpu.VMEM((rows, 1), F32),
            pltpu.VMEM((rows, 1), F32),
            pltpu.VMEM((rows, d), F32),
            pltpu.VMEM((N_HEADS, 128), F32),
        ],
    )
    return pl.pallas_call(
        functools.partial(_attn_sample_body, n_new, groups),
        grid_spec=grid_spec,
        out_shape=jax.ShapeDtypeStruct((n_seq, n_new, d), F32),
        compiler_params=pltpu.CompilerParams(
            dimension_semantics=("arbitrary", "arbitrary"),
            vmem_limit_bytes=VMEM_LIMIT),
        name="attn_sample",
    )(page_table, q, k_new, v_new, logf_new, cache_k, cache_v, cache_logf)


def _out_body(o_ref, gate_ref, x1_ref, w_ref, g_ref, y_ref):
    og = (o_ref[...].astype(F32) * gate_ref[...].astype(F32)).astype(BF16)
    x2 = x1_ref[...] + jnp.dot(og, w_ref[...], preferred_element_type=F32)
    y_ref[...] = _rms_norm(x2, g_ref[...])


def _out_proj(o, gate, x1, w, g):
    rows, d = x1.shape
    tile = min(ROW_TILE, rows)
    row_spec = pl.BlockSpec((tile, d), lambda i: (i, 0))
    return pl.pallas_call(
        _out_body,
        grid=(rows // tile,),
        in_specs=[row_spec, row_spec, row_spec, _const_spec(w.shape), _const_spec((1, d))],
        out_specs=row_spec,
        out_shape=jax.ShapeDtypeStruct((rows, d), F32),
        compiler_params=pltpu.CompilerParams(
            dimension_semantics=("parallel",), vmem_limit_bytes=VMEM_LIMIT),
        name="out_proj",
    )(o, gate, x1, w, g)


def kernel(x_prompt, x_sample, state_pool, cache_k, cache_v, cache_logf, page_table,
           norm_g, w_in_pool, pool_mix, pool_scale, w_out_pool,
           w_in_attn, b_forget, w_out_attn, final_norm):
    b, t, d = x_prompt.shape
    n_seq, n_new, _ = x_sample.shape
    n_phys = cache_k.shape[1]
    past = page_table.shape[1] * cache_k.shape[2]
    assert norm_g.shape[0] == 2 and d == D_MODEL and cache_k.shape[2] == PAGE

    g0 = norm_g[0].reshape(1, d)
    g1 = norm_g[1].reshape(1, d)
    gf = final_norm.reshape(1, d)
    win = w_in_pool[0].astype(BF16)
    mix = pool_mix[0].astype(BF16)
    scale = pool_scale[0].reshape(1, POOL_WIDTH)
    wout_pool = w_out_pool[0].astype(BF16)
    w_attn = w_in_attn[0, :, :4 * d].astype(BF16)
    wf = jnp.pad(w_in_attn[0, :, 4 * d:], ((0, 0), (0, 128 - N_HEADS))).astype(BF16)
    bf = jnp.pad(b_forget[0], (0, 128 - N_HEADS)).reshape(1, 128)
    wout_attn = w_out_attn[0].astype(BF16)

    x1p, last = _pool_prompt(x_prompt, g0, win, mix, scale, wout_pool)
    qp, kp, vp, k32, v32, gate_p, logf_p, bias_p = _proj_prompt(x1p, g1, w_attn, wf, bf)
    o_p = _attn_prompt(qp, kp, vp, bias_p)
    y_p = _out_proj(o_p.reshape(b * t, d), gate_p.reshape(b * t, d), x1p.reshape(b * t, d),
                    wout_attn, gf).reshape(b, t, d)

    xs_tm = x_sample.transpose(1, 0, 2).reshape(n_new * n_seq, d)
    st_tm = state_pool[0].transpose(1, 0, 2).reshape(POOL_CTX * n_seq, POOL_WIDTH)
    x1s_tm, u_tm = _pool_sample(xs_tm, st_tm, n_seq, n_new, past, g0, win, mix, scale, wout_pool)
    x1s = x1s_tm.reshape(n_new, n_seq, d).transpose(1, 0, 2).reshape(n_seq * n_new, d)
    u_new = u_tm.reshape(n_new, n_seq, POOL_WIDTH).transpose(1, 0, 2)
    qs, ks, vs, gate_s, logf_s = _proj_sample(x1s, g1, w_attn, wf, bf)
    o_s = _attn_sample(page_table,
                       qs.reshape(n_seq, n_new, d), ks.reshape(n_seq, n_new, d),
                       vs.reshape(n_seq, n_new, d), logf_s.reshape(n_seq, n_new, N_HEADS),
                       cache_k[0].reshape(n_phys, PAGE, d), cache_v[0].reshape(n_phys, PAGE, d),
                       cache_logf[0])
    y_s = _out_proj(o_s.reshape(n_seq * n_new, d).astype(BF16), gate_s, x1s, wout_attn, gf)

    pool_prompt = last[:, POOL_HALO - POOL_CTX:, :][None]
    pool_sample = jnp.concatenate([state_pool[0][:, n_new:, :], u_new], axis=1)[None]
    heads = (N_HEADS, HEAD_DIM)
    return (y_p, y_s.reshape(n_seq, n_new, d), pool_prompt, pool_sample,
            k32.reshape(1, b, t, *heads), v32.reshape(1, b, t, *heads), logf_p[None],
            ks.reshape(1, n_seq, n_new, *heads), vs.reshape(1, n_seq, n_new, *heads),
            logf_s.reshape(1, n_seq, n_new, N_HEADS))
```

```python
import functools

import jax
import jax.numpy as jnp
from jax import lax
from jax.experimental import pallas as pl
from jax.experimental.pallas import tpu as pltpu

F32 = jnp.float32
BF16 = jnp.bfloat16

D_MODEL = 1024
POOL_WIDTH = 2048
POOL_WINDOWS = (2, 4, 8, 16)
POOL_GROUP = POOL_WIDTH // len(POOL_WINDOWS)
POOL_CTX = max(POOL_WINDOWS) - 1
POOL_HALO = 16
N_HEADS = 16
HEAD_DIM = 64
HEAD_PAIRS = N_HEADS // 2
PAIR_WIDTH = 2 * HEAD_DIM
RMS_EPS = 1e-6
NEG_BIG = -1e30

ROW_TILE = 256
Q_TILE = 512
KV_TILE = 512
PAGE = 128
PAGES_PER_STEP = 8

VMEM_LIMIT = 56 * 1024 * 1024


def _rms_norm(x, g):
    ms = jnp.mean(x * x, axis=-1, keepdims=True)
    return x * lax.rsqrt(ms + RMS_EPS) * g


def _silu(z):
    return z * jax.nn.sigmoid(z)


def _log_sigmoid(x):
    return jnp.minimum(x, 0.0) - jnp.log1p(jnp.exp(-jnp.abs(x)))


def _const_spec(shape):
    return pl.BlockSpec(shape, lambda *_: (0,) * len(shape), pipeline_mode=pl.Buffered(1))


def _pool_tail(hb, pooled_fn, win_ref, mix_ref, scale_ref, wout_ref, resid):
    acc = resid
    for g in range(len(POOL_WINDOWS)):
        cols = slice(g * POOL_GROUP, (g + 1) * POOL_GROUP)
        zcols = slice(POOL_WIDTH + g * POOL_GROUP, POOL_WIDTH + (g + 1) * POOL_GROUP)
        u = jnp.dot(hb, win_ref[:, cols], preferred_element_type=F32)
        pooled = pooled_fn(g, u)
        y = jnp.dot(pooled.astype(BF16), mix_ref[g], preferred_element_type=F32)
        y = y * scale_ref[:, cols]
        z = jnp.dot(hb, win_ref[:, zcols], preferred_element_type=F32)
        yg = (y * _silu(z)).astype(BF16)
        acc = acc + jnp.dot(yg, wout_ref[cols, :], preferred_element_type=F32)
    return acc


def _pool_prompt_body(x_ref, g_ref, win_ref, mix_ref, scale_ref, wout_ref,
                      x1_ref, last_ref, halo_ref):
    t = pl.program_id(1)

    @pl.when(t == 0)
    def _():
        halo_ref[...] = jnp.zeros_like(halo_ref)

    x = x_ref[0]
    hb = _rms_norm(x, g_ref[...]).astype(BF16)
    pos = t * ROW_TILE + lax.broadcasted_iota(jnp.int32, (ROW_TILE, 1), 0)

    def pooled_fn(g, u):
        w = POOL_WINDOWS[g]
        cols = slice(g * POOL_GROUP, (g + 1) * POOL_GROUP)
        e = jnp.concatenate([halo_ref[:, cols], u], axis=0)
        tail = u[ROW_TILE - POOL_HALO:, :]
        halo_ref[:, cols] = tail
        last_ref[0, :, cols] = tail
        sh = 1
        while sh < w:
            e = e + pltpu.roll(e, sh, 0)
            sh *= 2
        inv = 1.0 / jnp.minimum(pos + 1, w).astype(F32)
        return e[POOL_HALO:, :] * inv - u

    x1_ref[0] = _pool_tail(hb, pooled_fn, win_ref, mix_ref, scale_ref, wout_ref, x)


def _pool_prompt(x, g, win, mix, scale, wout):
    b, t, d = x.shape
    grid = (b, t // ROW_TILE)
    return pl.pallas_call(
        _pool_prompt_body,
        grid=grid,
        in_specs=[
            pl.BlockSpec((1, ROW_TILE, d), lambda i, j: (i, j, 0)),
            _const_spec((1, d)),
            _const_spec(win.shape),
            _const_spec(mix.shape),
            _const_spec((1, POOL_WIDTH)),
            _const_spec(wout.shape),
        ],
        out_specs=[
            pl.BlockSpec((1, ROW_TILE, d), lambda i, j: (i, j, 0)),
            pl.BlockSpec((1, POOL_HALO, POOL_WIDTH), lambda i, j: (i, 0, 0)),
        ],
        out_shape=[
            jax.ShapeDtypeStruct((b, t, d), F32),
            jax.ShapeDtypeStruct((b, POOL_HALO, POOL_WIDTH), F32),
        ],
        scratch_shapes=[pltpu.VMEM((POOL_HALO, POOL_WIDTH), F32)],
        compiler_params=pltpu.CompilerParams(
            dimension_semantics=("parallel", "arbitrary"),
            vmem_limit_bytes=VMEM_LIMIT),
        name="pool_prompt",
    )(x, g, win, mix, scale, wout)


def _pool_sample_body(n_seq, n_new, start, x_ref, st_ref, g_ref, win_ref, mix_ref,
                      scale_ref, wout_ref, x1_ref, u_ref):
    x = x_ref[...]
    hb = _rms_norm(x, g_ref[...]).astype(BF16)

    def pooled_fn(g, u):
        w = POOL_WINDOWS[g]
        cols = slice(g * POOL_GROUP, (g + 1) * POOL_GROUP)
        u_ref[:, cols] = u

        def slab(e):
            if e < POOL_CTX:
                return st_ref[e * n_seq:(e + 1) * n_seq, cols]
            return u[(e - POOL_CTX) * n_seq:(e - POOL_CTX + 1) * n_seq, :]

        outs = []
        for t in range(n_new):
            s = slab(POOL_CTX + t)
            for e in range(POOL_CTX + t - w + 1, POOL_CTX + t):
                s = s + slab(e)
            cnt = min(start + t + 1, w)
            outs.append(s * (1.0 / cnt) - slab(POOL_CTX + t))
        return jnp.concatenate(outs, axis=0)

    x1_ref[...] = _pool_tail(hb, pooled_fn, win_ref, mix_ref, scale_ref, wout_ref, x)


def _pool_sample(x_tm, state_tm, n_seq, n_new, start, g, win, mix, scale, wout):
    rows, d = x_tm.shape
    return pl.pallas_call(
        functools.partial(_pool_sample_body, n_seq, n_new, start),
        out_shape=[
            jax.ShapeDtypeStruct((rows, d), F32),
            jax.ShapeDtypeStruct((rows, POOL_WIDTH), F32),
        ],
        compiler_params=pltpu.CompilerParams(vmem_limit_bytes=VMEM_LIMIT),
        name="pool_sample",
    )(x_tm, state_tm, g, win, mix, scale, wout)


def _project(x, g_ref, w_ref, wf_ref, bf_ref):
    hb = _rms_norm(x, g_ref[...]).astype(BF16)
    parts = [jnp.dot(hb, w_ref[:, i * D_MODEL:(i + 1) * D_MODEL], preferred_element_type=F32)
             for i in range(4)]
    f = jnp.dot(hb, wf_ref[...], preferred_element_type=F32) + bf_ref[...]
    q = parts[0] * (HEAD_DIM ** -0.5)
    return q, parts[1], parts[2], _silu(parts[3]), _log_sigmoid(f)


def _lane_cumsum(x):
    lane = lax.broadcasted_iota(jnp.int32, x.shape, 1)
    sh = 1
    while sh < x.shape[1]:
        x = x + jnp.where(lane >= sh, pltpu.roll(x, sh, 1), 0.0)
        sh *= 2
    return x


def _proj_prompt_body(x_ref, g_ref, w_ref, wf_ref, bf_ref,
                      q_ref, kp_ref, vp_ref, k_ref, v_ref, gate_ref, logf_ref, bias_ref,
                      carry_ref):
    t = pl.program_id(1)

    @pl.when(t == 0)
    def _():
        carry_ref[...] = jnp.zeros_like(carry_ref)

    q, k, v, gate, logf = _project(x_ref[0], g_ref, w_ref, wf_ref, bf_ref)
    k_ref[0] = k
    v_ref[0] = v
    gate_ref[0] = gate.astype(BF16)
    logf_ref[0] = logf[:, :N_HEADS]
    qb, kb, vb = q.astype(BF16), k.astype(BF16), v.astype(BF16)
    for p in range(HEAD_PAIRS):
        cols = slice(p * PAIR_WIDTH, (p + 1) * PAIR_WIDTH)
        q_ref[0, p] = qb[:, cols]
        kp_ref[0, p] = kb[:, cols]
        vp_ref[0, p] = vb[:, cols]

    lt = logf.T[:N_HEADS, :]
    carry = carry_ref[...]
    for c in range(ROW_TILE // 128):
        csum = _lane_cumsum(lt[:, c * 128:(c + 1) * 128]) + carry
        neg = -csum
        for p in range(HEAD_PAIRS):
            bias_ref[0, p, 0, :, c * 128:(c + 1) * 128] = neg[2 * p:2 * p + 2, :]
        carry = jnp.broadcast_to(csum[:, 127:128], carry.shape)
    carry_ref[...] = carry


def _proj_prompt(x1, g, w, wf, bf):
    b, t, d = x1.shape
    grid = (b, t // ROW_TILE)
    per_kv = KV_TILE // ROW_TILE
    row_spec = pl.BlockSpec((1, ROW_TILE, d), lambda i, j: (i, j, 0))
    pair_spec = pl.BlockSpec((1, HEAD_PAIRS, ROW_TILE, PAIR_WIDTH), lambda i, j: (i, 0, j, 0))
    pair_shape = jax.ShapeDtypeStruct((b, HEAD_PAIRS, t, PAIR_WIDTH), BF16)
    return pl.pallas_call(
        _proj_prompt_body,
        grid=grid,
        in_specs=[row_spec, _const_spec((1, d)), _const_spec(w.shape),
                  _const_spec(wf.shape), _const_spec(bf.shape)],
        out_specs=[
            pair_spec, pair_spec, pair_spec, row_spec, row_spec, row_spec,
            pl.BlockSpec((1, ROW_TILE, N_HEADS), lambda i, j: (i, j, 0)),
            pl.BlockSpec((1, HEAD_PAIRS, 1, 2, ROW_TILE),
                         lambda i, j: (i, 0, j // per_kv, 0, j % per_kv)),
        ],
        out_shape=[
            pair_shape, pair_shape, pair_shape,
            jax.ShapeDtypeStruct((b, t, d), F32),
            jax.ShapeDtypeStruct((b, t, d), F32),
            jax.ShapeDtypeStruct((b, t, d), BF16),
            jax.ShapeDtypeStruct((b, t, N_HEADS), F32),
            jax.ShapeDtypeStruct((b, HEAD_PAIRS, t // KV_TILE, 2, KV_TILE), F32),
        ],
        scratch_shapes=[pltpu.VMEM((N_HEADS, 128), F32)],
        compiler_params=pltpu.CompilerParams(
            dimension_semantics=("parallel", "arbitrary"),
            vmem_limit_bytes=VMEM_LIMIT),
        name="proj_prompt",
    )(x1, g, w, wf, bf)


def _proj_sample_body(x_ref, g_ref, w_ref, wf_ref, bf_ref,
                      q_ref, k_ref, v_ref, gate_ref, logf_ref):
    q, k, v, gate, logf = _project(x_ref[...], g_ref, w_ref, wf_ref, bf_ref)
    q_ref[...] = q.astype(BF16)
    k_ref[...] = k
    v_ref[...] = v
    gate_ref[...] = gate.astype(BF16)
    logf_ref[...] = logf[:, :N_HEADS]


def _proj_sample(x1, g, w, wf, bf):
    rows, d = x1.shape
    return pl.pallas_call(
        _proj_sample_body,
        out_shape=[
            jax.ShapeDtypeStruct((rows, d), BF16),
            jax.ShapeDtypeStruct((rows, d), F32),
            jax.ShapeDtypeStruct((rows, d), F32),
            jax.ShapeDtypeStruct((rows, d), BF16),
            jax.ShapeDtypeStruct((rows, N_HEADS), F32),
        ],
        compiler_params=pltpu.CompilerParams(vmem_limit_bytes=VMEM_LIMIT),
        name="proj_sample",
    )(x1, g, w, wf, bf)


def _attn_prompt_body(q_ref, k_ref, v_ref, bias_ref, o_ref):
    qi = pl.program_id(2)
    q = q_ref[0, 0]
    lane = lax.broadcasted_iota(jnp.int32, (1, PAIR_WIDTH), 1)
    first = lane < HEAD_DIM
    zero = jnp.zeros_like(q)
    q_heads = (jnp.where(first, q, zero), jnp.where(first, zero, q))
    row = lax.broadcasted_iota(jnp.int32, (Q_TILE, KV_TILE), 0)
    col = lax.broadcasted_iota(jnp.int32, (Q_TILE, KV_TILE), 1)
    visible = row >= col

    def tile(j, carry, diagonal):
        m0, l0, m1, l1, acc = carry
        start = pl.multiple_of(j * KV_TILE, KV_TILE)
        kt = k_ref[0, 0, pl.ds(start, KV_TILE), :]
        vt = v_ref[0, 0, pl.ds(start, KV_TILE), :]
        bt = bias_ref[0, 0, j]
        new = []
        for hh, (m, l) in enumerate(((m0, l0), (m1, l1))):
            s = lax.dot_general(q_heads[hh], kt, (((1,), (1,)), ((), ())),
                                preferred_element_type=F32)
            s = s + bt[hh:hh + 1, :]
            if diagonal:
                s = jnp.where(visible, s, NEG_BIG)
            m_new = jnp.maximum(m, jnp.max(s, axis=1, keepdims=True))
            alpha = jnp.exp(m - m_new)
            p = jnp.exp(s - m_new)
            l_new = alpha * l + jnp.sum(p, axis=1, keepdims=True)
            pv = jnp.dot(p.astype(BF16), vt, preferred_element_type=F32)
            new.append((m_new, l_new, alpha, pv))
        alpha = jnp.where(first, new[0][2], new[1][2])
        pv = jnp.where(first, new[0][3], new[1][3])
        return new[0][0], new[0][1], new[1][0], new[1][1], acc * alpha + pv

    neg = jnp.full((Q_TILE, 1), NEG_BIG, F32)
    zcol = jnp.zeros((Q_TILE, 1), F32)
    init = (neg, zcol, neg, zcol, jnp.zeros((Q_TILE, PAIR_WIDTH), F32))
    carry = lax.fori_loop(0, qi, lambda j, c: tile(j, c, False), init)
    _, l0, _, l1, acc = tile(qi, carry, True)
    o_ref[0] = (acc * jnp.where(first, 1.0 / l0, 1.0 / l1)).astype(BF16)


def _attn_prompt(q, k, v, bias):
    b, hp, t, w = q.shape
    grid = (b, hp, t // Q_TILE)
    return pl.pallas_call(
        _attn_prompt_body,
        grid=grid,
        in_specs=[
            pl.BlockSpec((1, 1, Q_TILE, w), lambda i, p, j: (i, p, j, 0)),
            pl.BlockSpec((1, 1, t, w), lambda i, p, j: (i, p, 0, 0)),
            pl.BlockSpec((1, 1, t, w), lambda i, p, j: (i, p, 0, 0)),
            pl.BlockSpec((1, 1, t // KV_TILE, 2, KV_TILE), lambda i, p, j: (i, p, 0, 0, 0)),
        ],
        out_specs=pl.BlockSpec((1, Q_TILE, w), lambda i, p, j: (i, j, p)),
        out_shape=jax.ShapeDtypeStruct((b, t, hp * w), BF16),
        compiler_params=pltpu.CompilerParams(
            dimension_semantics=("parallel", "parallel", "arbitrary"),
            vmem_limit_bytes=VMEM_LIMIT),
        name="attn_prompt",
    )(q, k, v, bias)


def _attn_sample_body(n_new, groups, pt_ref, q_ref, kn_ref, vn_ref, ln_ref,
                      ck_hbm, cv_hbm, cl_hbm, o_ref,
                      kbuf, vbuf, lbuf, sems, qbd_ref, kpad, vpad, lpad,
                      m_ref, l_ref, acc_ref, carry_ref):
    b = pl.program_id(0)
    gi = pl.program_id(1)
    n_b = pl.num_programs(0)
    step = b * groups + gi
    slot = step % 2
    rows = n_new * N_HEADS

    def copies(bb, gg, sl):
        out = []
        for p in range(PAGES_PER_STEP):
            page = pt_ref[bb, (groups - 1 - gg) * PAGES_PER_STEP + p]
            out.append(pltpu.make_async_copy(ck_hbm.at[page], kbuf.at[sl, p], sems.at[0, sl]))
            out.append(pltpu.make_async_copy(cv_hbm.at[page], vbuf.at[sl, p], sems.at[1, sl]))
            out.append(pltpu.make_async_copy(cl_hbm.at[page], lbuf.at[sl, p], sems.at[2, sl]))
        return out

    @pl.when(step == 0)
    def _():
        for c in copies(0, 0, 0):
            c.start()
        kpad[...] = jnp.zeros_like(kpad)
        vpad[...] = jnp.zeros_like(vpad)
        lpad[...] = jnp.zeros_like(lpad)

    @pl.when(step + 1 < n_b * groups)
    def _():
        nxt = step + 1
        for c in copies(nxt // groups, nxt % groups, 1 - slot):
            c.start()

    row_head = lax.broadcasted_iota(jnp.int32, (N_HEADS, D_MODEL), 0)
    lane_head = lax.broadcasted_iota(jnp.int32, (N_HEADS, D_MODEL), 1) // HEAD_DIM
    own = row_head == lane_head

    def suffix_bias(lp_t, carry):
        incl = _lane_cumsum(lp_t)
        total = jnp.broadcast_to(incl[:, 127:128], incl.shape)
        return total - incl + carry, carry + total

    def accumulate(s, pv_fn):
        m_old = m_ref[...]
        m_new = jnp.maximum(m_old, jnp.max(s, axis=1, keepdims=True))
        alpha = jnp.exp(m_old - m_new)
        p = jnp.exp(s - m_new)
        l_ref[...] = alpha * l_ref[...] + jnp.sum(p, axis=1, keepdims=True)
        acc_ref[...] = acc_ref[...] * alpha + pv_fn(p.astype(BF16))
        m_ref[...] = m_new

    @pl.when(gi == 0)
    def _():
        q = q_ref[0].astype(F32)
        for t in range(n_new):
            qt = jnp.broadcast_to(q[t:t + 1, :], (N_HEADS, D_MODEL))
            qbd_ref[t * N_HEADS:(t + 1) * N_HEADS, :] = jnp.where(own, qt, 0.0).astype(BF16)
        m_ref[...] = jnp.full_like(m_ref, NEG_BIG)
        l_ref[...] = jnp.zeros_like(l_ref)
        acc_ref[...] = jnp.zeros_like(acc_ref)
        kpad[0:n_new, :] = kn_ref[0]
        vpad[0:n_new, :] = vn_ref[0]
        lpad[0:n_new, :] = ln_ref[0]
        eye = (lax.broadcasted_iota(jnp.int32, (N_HEADS, N_HEADS), 0) ==
               lax.broadcasted_iota(jnp.int32, (N_HEADS, N_HEADS), 1)).astype(F32)
        lp_t = lax.dot_general(eye, lpad[...], (((1,), (1,)), ((), ())),
                               precision=lax.Precision.HIGHEST, preferred_element_type=F32)
        bias, carry = suffix_bias(lp_t, jnp.zeros((N_HEADS, 128), F32))
        carry_ref[...] = carry
        s = lax.dot_general(qbd_ref[...], kpad[...].astype(BF16), (((1,), (1,)), ((), ())),
                            preferred_element_type=F32)
        s = s + jnp.concatenate([bias] * n_new, axis=0)
        q_idx = lax.broadcasted_iota(jnp.int32, (rows, PAGE), 0) // N_HEADS
        k_idx = lax.broadcasted_iota(jnp.int32, (rows, PAGE), 1)
        s = jnp.where(k_idx <= q_idx, s, NEG_BIG)
        vb = vpad[...].astype(BF16)
        accumulate(s, lambda p: jnp.dot(p, vb, preferred_element_type=F32))

    for c in copies(b, gi, slot):
        c.wait()

    carry = carry_ref[...]
    biases = [None] * PAGES_PER_STEP
    for p in reversed(range(PAGES_PER_STEP)):
        biases[p], carry = suffix_bias(lbuf[slot, p], carry)
    carry_ref[...] = carry
    bias = jnp.concatenate(biases, axis=1)
    kcat = jnp.concatenate([kbuf[slot, p].astype(BF16) for p in range(PAGES_PER_STEP)], axis=1)
    s = jnp.dot(qbd_ref[...], kcat, preferred_element_type=F32)
    s = s + jnp.concatenate([bias] * n_new, axis=0)
    vcat = jnp.concatenate([vbuf[slot, p].astype(BF16) for p in range(PAGES_PER_STEP)], axis=1)
    accumulate(s, lambda p: lax.dot_general(p, vcat, (((1,), (1,)), ((), ())),
                                            preferred_element_type=F32))

    @pl.when(gi == groups - 1)
    def _():
        acc = acc_ref[...] * (1.0 / l_ref[...])
        for t in range(n_new):
            blk = jnp.where(own, acc[t * N_HEADS:(t + 1) * N_HEADS, :], 0.0)
            o_ref[0, t:t + 1, :] = jnp.sum(blk, axis=0, keepdims=True)


def _attn_sample(page_table, q, k_new, v_new, logf_new, cache_kt, cache_vt, cache_lt):
    n_seq, n_new, d = q.shape
    n_pages = page_table.shape[1]
    groups = n_pages // PAGES_PER_STEP
    rows = n_new * N_HEADS
    new_spec = pl.BlockSpec((1, n_new, d), lambda i, j, pt: (i, 0, 0))
    any_spec = pl.BlockSpec(memory_space=pl.ANY)
    grid_spec = pltpu.PrefetchScalarGridSpec(
        num_scalar_prefetch=1,
        grid=(n_seq, groups),
        in_specs=[new_spec, new_spec, new_spec,
                  pl.BlockSpec((1, n_new, N_HEADS), lambda i, j, pt: (i, 0, 0)),
                  any_spec, any_spec, any_spec],
        out_specs=new_spec,
        scratch_shapes=[
            pltpu.VMEM((2, PAGES_PER_STEP, d, PAGE), F32),
            pltpu.VMEM((2, PAGES_PER_STEP, d, PAGE), F32),
            pltpu.VMEM((2, PAGES_PER_STEP, N_HEADS, PAGE), F32),
            pltpu.SemaphoreType.DMA((3, 2)),
            pltpu.VMEM((rows, d), BF16),
            pltpu.VMEM((PAGE, d), F32),
            pltpu.VMEM((PAGE, d), F32),
            pltpu.VMEM((PAGE, N_HEADS), F32),
            pltpu.VMEM((rows, 1), F32),
            pltpu.VMEM((rows, 1), F32),
            pltpu.VMEM((rows, d), F32),
            pltpu.VMEM((N_HEADS, 128), F32),
        ],
    )
    return pl.pallas_call(
        functools.partial(_attn_sample_body, n_new, groups),
        grid_spec=grid_spec,
        out_shape=jax.ShapeDtypeStruct((n_seq, n_new, d), F32),
        compiler_params=pltpu.CompilerParams(
            dimension_semantics=("arbitrary", "arbitrary"),
            vmem_limit_bytes=VMEM_LIMIT),
        name="attn_sample",
    )(page_table, q, k_new, v_new, logf_new, cache_kt, cache_vt, cache_lt)


def _out_body(o_ref, gate_ref, x1_ref, w_ref, g_ref, y_ref):
    og = (o_ref[...].astype(F32) * gate_ref[...].astype(F32)).astype(BF16)
    x2 = x1_ref[...] + jnp.dot(og, w_ref[...], preferred_element_type=F32)
    y_ref[...] = _rms_norm(x2, g_ref[...])


def _out_proj(o, gate, x1, w, g):
    rows, d = x1.shape
    tile = min(ROW_TILE, rows)
    row_spec = pl.BlockSpec((tile, d), lambda i: (i, 0))
    return pl.pallas_call(
        _out_body,
        grid=(rows // tile,),
        in_specs=[row_spec, row_spec, row_spec, _const_spec(w.shape), _const_spec((1, d))],
        out_specs=row_spec,
        out_shape=jax.ShapeDtypeStruct((rows, d), F32),
        compiler_params=pltpu.CompilerParams(
            dimension_semantics=("parallel",), vmem_limit_bytes=VMEM_LIMIT),
        name="out_proj",
    )(o, gate, x1, w, g)


def kernel(x_prompt, x_sample, state_pool, cache_k, cache_v, cache_logf, page_table,
           norm_g, w_in_pool, pool_mix, pool_scale, w_out_pool,
           w_in_attn, b_forget, w_out_attn, final_norm):
    b, t, d = x_prompt.shape
    n_seq, n_new, _ = x_sample.shape
    n_phys = cache_k.shape[1]
    past = page_table.shape[1] * cache_k.shape[2]
    assert norm_g.shape[0] == 2 and d == D_MODEL and cache_k.shape[2] == PAGE

    g0 = norm_g[0].reshape(1, d)
    g1 = norm_g[1].reshape(1, d)
    gf = final_norm.reshape(1, d)
    win = w_in_pool[0].astype(BF16)
    mix = pool_mix[0].astype(BF16)
    scale = pool_scale[0].reshape(1, POOL_WIDTH)
    wout_pool = w_out_pool[0].astype(BF16)
    w_attn = w_in_attn[0, :, :4 * d].astype(BF16)
    wf = jnp.pad(w_in_attn[0, :, 4 * d:], ((0, 0), (0, 128 - N_HEADS))).astype(BF16)
    bf = jnp.pad(b_forget[0], (0, 128 - N_HEADS)).reshape(1, 128)
    wout_attn = w_out_attn[0].astype(BF16)

    x1p, last = _pool_prompt(x_prompt, g0, win, mix, scale, wout_pool)
    qp, kp, vp, k32, v32, gate_p, logf_p, bias_p = _proj_prompt(x1p, g1, w_attn, wf, bf)
    o_p = _attn_prompt(qp, kp, vp, bias_p)
    y_p = _out_proj(o_p.reshape(b * t, d), gate_p.reshape(b * t, d), x1p.reshape(b * t, d),
                    wout_attn, gf).reshape(b, t, d)

    xs_tm = x_sample.transpose(1, 0, 2).reshape(n_new * n_seq, d)
    st_tm = state_pool[0].transpose(1, 0, 2).reshape(POOL_CTX * n_seq, POOL_WIDTH)
    x1s_tm, u_tm = _pool_sample(xs_tm, st_tm, n_seq, n_new, past, g0, win, mix, scale, wout_pool)
    x1s = x1s_tm.reshape(n_new, n_seq, d).transpose(1, 0, 2).reshape(n_seq * n_new, d)
    u_new = u_tm.reshape(n_new, n_seq, POOL_WIDTH).transpose(1, 0, 2)
    qs, ks, vs, gate_s, logf_s = _proj_sample(x1s, g1, w_attn, wf, bf)
    o_s = _attn_sample(page_table,
                       qs.reshape(n_seq, n_new, d), ks.reshape(n_seq, n_new, d),
                       vs.reshape(n_seq, n_new, d), logf_s.reshape(n_seq, n_new, N_HEADS),
                       cache_k[0].transpose(0, 2, 3, 1).reshape(n_phys, d, PAGE),
                       cache_v[0].transpose(0, 2, 3, 1).reshape(n_phys, d, PAGE),
                       cache_logf[0].transpose(0, 2, 1))
    y_s = _out_proj(o_s.reshape(n_seq * n_new, d).astype(BF16), gate_s, x1s, wout_attn, gf)

    pool_prompt = last[:, POOL_HALO - POOL_CTX:, :][None]
    pool_sample = jnp.concatenate([state_pool[0][:, n_new:, :], u_new], axis=1)[None]
    heads = (N_HEADS, HEAD_DIM)
    return (y_p, y_s.reshape(n_seq, n_new, d), pool_prompt, pool_sample,
            k32.reshape(1, b, t, *heads), v32.reshape(1, b, t, *heads), logf_p[None],
            ks.reshape(1, n_seq, n_new, *heads), vs.reshape(1, n_seq, n_new, *heads),
            logf_s.reshape(1, n_seq, n_new, N_HEADS))
```

```python
import functools

import jax
import jax.numpy as jnp
from jax import lax
from jax.experimental import pallas as pl
from jax.experimental.pallas import tpu as pltpu

F32 = jnp.float32
BF16 = jnp.bfloat16

D_MODEL = 1024
POOL_WIDTH = 2048
POOL_WINDOWS = (2, 4, 8, 16)
POOL_GROUP = POOL_WIDTH // len(POOL_WINDOWS)
POOL_CTX = max(POOL_WINDOWS) - 1
POOL_HALO = 16
N_HEADS = 16
HEAD_DIM = 64
HEAD_PAIRS = N_HEADS // 2
PAIR_WIDTH = 2 * HEAD_DIM
RMS_EPS = 1e-6
NEG_BIG = -1e30
LOG2E = 1.4426950408889634

ROW_TILE = 512
OUT_TILE = 1024
Q_TILE = 1024
KV_TILE = 512
PAGE = 128
PAGES_PER_STEP = 8

VMEM_LIMIT = 56 * 1024 * 1024


def _rms_norm(x, g):
    ms = jnp.mean(x * x, axis=-1, keepdims=True)
    return x * lax.rsqrt(ms + RMS_EPS) * g


def _silu(z):
    return z * jax.nn.sigmoid(z)


def _log_sigmoid(x):
    return jnp.minimum(x, 0.0) - jnp.log1p(jnp.exp(-jnp.abs(x)))


def _const_spec(shape):
    return pl.BlockSpec(shape, lambda *_: (0,) * len(shape), pipeline_mode=pl.Buffered(1))


def _pool_tail(hb, pooled_fn, win_ref, mix_ref, scale_ref, wout_ref, resid):
    acc = resid
    for g in range(len(POOL_WINDOWS)):
        cols = slice(g * POOL_GROUP, (g + 1) * POOL_GROUP)
        zcols = slice(POOL_WIDTH + g * POOL_GROUP, POOL_WIDTH + (g + 1) * POOL_GROUP)
        u = jnp.dot(hb, win_ref[:, cols], preferred_element_type=F32)
        pooled = pooled_fn(g, u)
        y = jnp.dot(pooled.astype(BF16), mix_ref[g], preferred_element_type=F32)
        y = y * scale_ref[:, cols]
        z = jnp.dot(hb, win_ref[:, zcols], preferred_element_type=F32)
        yg = (y * _silu(z)).astype(BF16)
        acc = acc + jnp.dot(yg, wout_ref[cols, :], preferred_element_type=F32)
    return acc


def _pool_prompt_body(x_ref, g_ref, win_ref, mix_ref, scale_ref, wout_ref,
                      x1_ref, last_ref, halo_ref):
    t = pl.program_id(1)

    @pl.when(t == 0)
    def _():
        halo_ref[...] = jnp.zeros_like(halo_ref)

    x = x_ref[0]
    hb = _rms_norm(x, g_ref[...]).astype(BF16)
    pos = t * ROW_TILE + lax.broadcasted_iota(jnp.int32, (ROW_TILE, 1), 0)

    def pooled_fn(g, u):
        w = POOL_WINDOWS[g]
        cols = slice(g * POOL_GROUP, (g + 1) * POOL_GROUP)
        e = jnp.concatenate([halo_ref[:, cols], u], axis=0)
        tail = u[ROW_TILE - POOL_HALO:, :]
        halo_ref[:, cols] = tail
        last_ref[0, :, cols] = tail
        sh = 1
        while sh < w:
            e = e + pltpu.roll(e, sh, 0)
            sh *= 2
        inv = 1.0 / jnp.minimum(pos + 1, w).astype(F32)
        return e[POOL_HALO:, :] * inv - u

    x1_ref[0] = _pool_tail(hb, pooled_fn, win_ref, mix_ref, scale_ref, wout_ref, x)


def _pool_prompt(x, g, win, mix, scale, wout):
    b, t, d = x.shape
    grid = (b, t // ROW_TILE)
    return pl.pallas_call(
        _pool_prompt_body,
        grid=grid,
        in_specs=[
            pl.BlockSpec((1, ROW_TILE, d), lambda i, j: (i, j, 0)),
            _const_spec((1, d)),
            _const_spec(win.shape),
            _const_spec(mix.shape),
            _const_spec((1, POOL_WIDTH)),
            _const_spec(wout.shape),
        ],
        out_specs=[
            pl.BlockSpec((1, ROW_TILE, d), lambda i, j: (i, j, 0)),
            pl.BlockSpec((1, POOL_HALO, POOL_WIDTH), lambda i, j: (i, 0, 0)),
        ],
        out_shape=[
            jax.ShapeDtypeStruct((b, t, d), F32),
            jax.ShapeDtypeStruct((b, POOL_HALO, POOL_WIDTH), F32),
        ],
        scratch_shapes=[pltpu.VMEM((POOL_HALO, POOL_WIDTH), F32)],
        compiler_params=pltpu.CompilerParams(
            dimension_semantics=("parallel", "arbitrary"),
            vmem_limit_bytes=VMEM_LIMIT),
        name="pool_prompt",
    )(x, g, win, mix, scale, wout)


def _pool_sample_body(n_seq, n_new, start, x_ref, st_ref, g_ref, win_ref, mix_ref,
                      scale_ref, wout_ref, x1_ref, u_ref):
    x = x_ref[...]
    hb = _rms_norm(x, g_ref[...]).astype(BF16)

    def pooled_fn(g, u):
        w = POOL_WINDOWS[g]
        cols = slice(g * POOL_GROUP, (g + 1) * POOL_GROUP)
        u_ref[:, cols] = u

        def slab(e):
            if e < POOL_CTX:
                return st_ref[e * n_seq:(e + 1) * n_seq, cols]
            return u[(e - POOL_CTX) * n_seq:(e - POOL_CTX + 1) * n_seq, :]

        outs = []
        for t in range(n_new):
            s = slab(POOL_CTX + t)
            for e in range(POOL_CTX + t - w + 1, POOL_CTX + t):
                s = s + slab(e)
            cnt = min(start + t + 1, w)
            outs.append(s * (1.0 / cnt) - slab(POOL_CTX + t))
        return jnp.concatenate(outs, axis=0)

    x1_ref[...] = _pool_tail(hb, pooled_fn, win_ref, mix_ref, scale_ref, wout_ref, x)


def _pool_sample(x_tm, state_tm, n_seq, n_new, start, g, win, mix, scale, wout):
    rows, d = x_tm.shape
    return pl.pallas_call(
        functools.partial(_pool_sample_body, n_seq, n_new, start),
        out_shape=[
            jax.ShapeDtypeStruct((rows, d), F32),
            jax.ShapeDtypeStruct((rows, POOL_WIDTH), F32),
        ],
        compiler_params=pltpu.CompilerParams(vmem_limit_bytes=VMEM_LIMIT),
        name="pool_sample",
    )(x_tm, state_tm, g, win, mix, scale, wout)


def _project(x, g_ref, w_ref, wf_ref, bf_ref):
    hb = _rms_norm(x, g_ref[...]).astype(BF16)
    parts = [jnp.dot(hb, w_ref[:, i * D_MODEL:(i + 1) * D_MODEL], preferred_element_type=F32)
             for i in range(4)]
    f = jnp.dot(hb, wf_ref[...], preferred_element_type=F32) + bf_ref[...]
    q = parts[0] * (LOG2E * HEAD_DIM ** -0.5)
    return q, parts[1], parts[2], _silu(parts[3]), _log_sigmoid(f)


def _lane_cumsum(x):
    lane = lax.broadcasted_iota(jnp.int32, x.shape, 1)
    sh = 1
    while sh < x.shape[1]:
        x = x + jnp.where(lane >= sh, pltpu.roll(x, sh, 1), 0.0)
        sh *= 2
    return x


def _proj_prompt_body(x_ref, g_ref, w_ref, wf_ref, bf_ref,
                      q_ref, kp_ref, vp_ref, k_ref, v_ref, gate_ref, logf_ref, bias_ref,
                      carry_ref):
    t = pl.program_id(1)

    @pl.when(t == 0)
    def _():
        carry_ref[...] = jnp.zeros_like(carry_ref)

    q, k, v, gate, logf = _project(x_ref[0], g_ref, w_ref, wf_ref, bf_ref)
    k_ref[0] = k
    v_ref[0] = v
    gate_ref[0] = gate.astype(BF16)
    logf_ref[0] = logf[:, :N_HEADS]
    qb, kb, vb = q.astype(BF16), k.astype(BF16), v.astype(BF16)
    for p in range(HEAD_PAIRS):
        cols = slice(p * PAIR_WIDTH, (p + 1) * PAIR_WIDTH)
        q_ref[0, p] = qb[:, cols]
        kp_ref[0, p] = kb[:, cols]
        vp_ref[0, p] = vb[:, cols]

    lt = logf.T[:N_HEADS, :]
    carry = carry_ref[...]
    for c in range(ROW_TILE // 128):
        csum = _lane_cumsum(lt[:, c * 128:(c + 1) * 128]) + carry
        neg = csum * (-LOG2E)
        for p in range(HEAD_PAIRS):
            bias_ref[0, p, 0, :, c * 128:(c + 1) * 128] = neg[2 * p:2 * p + 2, :]
        carry = jnp.broadcast_to(csum[:, 127:128], carry.shape)
    carry_ref[...] = carry


def _proj_prompt(x1, g, w, wf, bf):
    b, t, d = x1.shape
    grid = (b, t // ROW_TILE)
    per_kv = KV_TILE // ROW_TILE
    row_spec = pl.BlockSpec((1, ROW_TILE, d), lambda i, j: (i, j, 0))
    pair_spec = pl.BlockSpec((1, HEAD_PAIRS, ROW_TILE, PAIR_WIDTH), lambda i, j: (i, 0, j, 0))
    pair_shape = jax.ShapeDtypeStruct((b, HEAD_PAIRS, t, PAIR_WIDTH), BF16)
    return pl.pallas_call(
        _proj_prompt_body,
        grid=grid,
        in_specs=[row_spec, _const_spec((1, d)), _const_spec(w.shape),
                  _const_spec(wf.shape), _const_spec(bf.shape)],
        out_specs=[
            pair_spec, pair_spec, pair_spec, row_spec, row_spec, row_spec,
            pl.BlockSpec((1, ROW_TILE, N_HEADS), lambda i, j: (i, j, 0)),
            pl.BlockSpec((1, HEAD_PAIRS, 1, 2, ROW_TILE),
                         lambda i, j: (i, 0, j // per_kv, 0, j % per_kv)),
        ],
        out_shape=[
            pair_shape, pair_shape, pair_shape,
            jax.ShapeDtypeStruct((b, t, d), F32),
            jax.ShapeDtypeStruct((b, t, d), F32),
            jax.ShapeDtypeStruct((b, t, d), BF16),
            jax.ShapeDtypeStruct((b, t, N_HEADS), F32),
            jax.ShapeDtypeStruct((b, HEAD_PAIRS, t // KV_TILE, 2, KV_TILE), F32),
        ],
        scratch_shapes=[pltpu.VMEM((N_HEADS, 128), F32)],
        compiler_params=pltpu.CompilerParams(
            dimension_semantics=("parallel", "arbitrary"),
            vmem_limit_bytes=VMEM_LIMIT),
        name="proj_prompt",
    )(x1, g, w, wf, bf)


def _proj_sample_body(x_ref, g_ref, w_ref, wf_ref, bf_ref,
                      q_ref, k_ref, v_ref, gate_ref, logf_ref):
    q, k, v, gate, logf = _project(x_ref[...], g_ref, w_ref, wf_ref, bf_ref)
    q_ref[...] = q.astype(BF16)
    k_ref[...] = k
    v_ref[...] = v
    gate_ref[...] = gate.astype(BF16)
    logf_ref[...] = logf[:, :N_HEADS]


def _proj_sample(x1, g, w, wf, bf):
    rows, d = x1.shape
    return pl.pallas_call(
        _proj_sample_body,
        out_shape=[
            jax.ShapeDtypeStruct((rows, d), BF16),
            jax.ShapeDtypeStruct((rows, d), F32),
            jax.ShapeDtypeStruct((rows, d), F32),
            jax.ShapeDtypeStruct((rows, d), BF16),
            jax.ShapeDtypeStruct((rows, N_HEADS), F32),
        ],
        compiler_params=pltpu.CompilerParams(vmem_limit_bytes=VMEM_LIMIT),
        name="proj_sample",
    )(x1, g, w, wf, bf)


def _attn_prompt_body(q_ref, k_ref, v_ref, bias_ref, o_ref):
    qi = pl.program_id(2)
    q = q_ref[0, 0]
    lane = lax.broadcasted_iota(jnp.int32, (1, PAIR_WIDTH), 1)
    first = lane < HEAD_DIM
    zero = jnp.zeros_like(q)
    q_heads = (jnp.where(first, q, zero), jnp.where(first, zero, q))
    half = Q_TILE // 2
    assert half == KV_TILE
    causal = (lax.broadcasted_iota(jnp.int32, (half, KV_TILE), 0) >=
              lax.broadcasted_iota(jnp.int32, (half, KV_TILE), 1))

    def head_update(qh, kt, vt, brow, m, l, mask_rows):
        s = lax.dot_general(qh, kt, (((1,), (1,)), ((), ())), preferred_element_type=F32)
        s = s + brow
        if mask_rows is not None:
            top = jnp.where(causal, s[:mask_rows], NEG_BIG)
            s = top if mask_rows == s.shape[0] else jnp.concatenate([top, s[mask_rows:]], axis=0)
        m_new = jnp.maximum(m, jnp.max(s, axis=1, keepdims=True))
        alpha = jnp.exp2(m - m_new)
        p = jnp.exp2(s - m_new)
        l_new = alpha * l + jnp.sum(p, axis=1, keepdims=True)
        pv = jnp.dot(p.astype(BF16), vt, preferred_element_type=F32)
        return m_new, l_new, alpha, pv

    def key_tile(t, carry, rows, mask_rows):
        m0, l0, m1, l1, acc = carry
        start = pl.multiple_of(t * KV_TILE, KV_TILE)
        kt = k_ref[0, 0, pl.ds(start, KV_TILE), :]
        vt = v_ref[0, 0, pl.ds(start, KV_TILE), :]
        bt = bias_ref[0, 0, t]
        m0, l0, a0, pv0 = head_update(q_heads[0][rows], kt, vt, bt[0:1, :], m0, l0, mask_rows)
        m1, l1, a1, pv1 = head_update(q_heads[1][rows], kt, vt, bt[1:2, :], m1, l1, mask_rows)
        acc = acc * jnp.where(first, a0, a1) + jnp.where(first, pv0, pv1)
        return m0, l0, m1, l1, acc

    everything = slice(0, Q_TILE)

    def visible_pair(i, carry):
        carry = key_tile(2 * i, carry, everything, None)
        return key_tile(2 * i + 1, carry, everything, None)

    neg = jnp.full((Q_TILE, 1), NEG_BIG, F32)
    zcol = jnp.zeros((Q_TILE, 1), F32)
    init = (neg, zcol, neg, zcol, jnp.zeros((Q_TILE, PAIR_WIDTH), F32))
    carry = lax.fori_loop(0, qi, visible_pair, init)
    m0, l0, m1, l1, acc = key_tile(2 * qi, carry, everything, half)
    lower = slice(half, Q_TILE)
    low = key_tile(2 * qi + 1, (m0[lower], l0[lower], m1[lower], l1[lower], acc[lower]),
                   lower, half)
    o_ref[0, :half, :] = (acc[:half] * jnp.where(first, 1.0 / l0[:half], 1.0 / l1[:half])
                          ).astype(BF16)
    o_ref[0, half:, :] = (low[4] * jnp.where(first, 1.0 / low[1], 1.0 / low[3])).astype(BF16)


def _attn_prompt(q, k, v, bias):
    b, hp, t, w = q.shape
    grid = (b, hp, t // Q_TILE)
    return pl.pallas_call(
        _attn_prompt_body,
        grid=grid,
        in_specs=[
            pl.BlockSpec((1, 1, Q_TILE, w), lambda i, p, j: (i, p, j, 0)),
            pl.BlockSpec((1, 1, t, w), lambda i, p, j: (i, p, 0, 0)),
            pl.BlockSpec((1, 1, t, w), lambda i, p, j: (i, p, 0, 0)),
            pl.BlockSpec((1, 1, t // KV_TILE, 2, KV_TILE), lambda i, p, j: (i, p, 0, 0, 0)),
        ],
        out_specs=pl.BlockSpec((1, Q_TILE, w), lambda i, p, j: (i, j, p)),
        out_shape=jax.ShapeDtypeStruct((b, t, hp * w), BF16),
        compiler_params=pltpu.CompilerParams(
            dimension_semantics=("parallel", "parallel", "arbitrary"),
            vmem_limit_bytes=VMEM_LIMIT),
        name="attn_prompt",
    )(q, k, v, bias)


def _attn_sample_body(n_new, groups, pt_ref, q_ref, kn_ref, vn_ref, ln_ref,
                      ck_hbm, cv_hbm, cl_hbm, o_ref,
                      kbuf, vbuf, lbuf, sems, qbd_ref, kpad, vpad, lpad,
                      m_ref, l_ref, acc_ref, carry_ref):
    b = pl.program_id(0)
    gi = pl.program_id(1)
    n_b = pl.num_programs(0)
    step = b * groups + gi
    slot = step % 2
    rows = n_new * N_HEADS

    def copies(bb, gg, sl):
        out = []
        for p in range(PAGES_PER_STEP):
            page = pt_ref[bb, (groups - 1 - gg) * PAGES_PER_STEP + p]
            out.append(pltpu.make_async_copy(ck_hbm.at[page], kbuf.at[sl, p], sems.at[0, sl]))
            out.append(pltpu.make_async_copy(cv_hbm.at[page], vbuf.at[sl, p], sems.at[1, sl]))
            out.append(pltpu.make_async_copy(cl_hbm.at[page], lbuf.at[sl, p], sems.at[2, sl]))
        return out

    @pl.when(step == 0)
    def _():
        for c in copies(0, 0, 0):
            c.start()
        kpad[...] = jnp.zeros_like(kpad)
        vpad[...] = jnp.zeros_like(vpad)
        lpad[...] = jnp.zeros_like(lpad)

    @pl.when(step + 1 < n_b * groups)
    def _():
        nxt = step + 1
        for c in copies(nxt // groups, nxt % groups, 1 - slot):
            c.start()

    row_head = lax.broadcasted_iota(jnp.int32, (N_HEADS, D_MODEL), 0)
    lane_head = lax.broadcasted_iota(jnp.int32, (N_HEADS, D_MODEL), 1) // HEAD_DIM
    own = row_head == lane_head

    def suffix_bias(lp_t, carry):
        incl = _lane_cumsum(lp_t)
        total = jnp.broadcast_to(incl[:, 127:128], incl.shape)
        return total - incl + carry, carry + total

    def accumulate(s, pv_fn):
        m_old = m_ref[...]
        m_new = jnp.maximum(m_old, jnp.max(s, axis=1, keepdims=True))
        alpha = jnp.exp2(m_old - m_new)
        p = jnp.exp2(s - m_new)
        l_ref[...] = alpha * l_ref[...] + jnp.sum(p, axis=1, keepdims=True)
        acc_ref[...] = acc_ref[...] * alpha + pv_fn(p.astype(BF16))
        m_ref[...] = m_new

    @pl.when(gi == 0)
    def _():
        q = q_ref[0].astype(F32)
        for t in range(n_new):
            qt = jnp.broadcast_to(q[t:t + 1, :], (N_HEADS, D_MODEL))
            qbd_ref[t * N_HEADS:(t + 1) * N_HEADS, :] = jnp.where(own, qt, 0.0).astype(BF16)
        m_ref[...] = jnp.full_like(m_ref, NEG_BIG)
        l_ref[...] = jnp.zeros_like(l_ref)
        acc_ref[...] = jnp.zeros_like(acc_ref)
        kpad[0:n_new, :] = kn_ref[0]
        vpad[0:n_new, :] = vn_ref[0]
        lpad[0:n_new, :] = ln_ref[0]
        eye = (lax.broadcasted_iota(jnp.int32, (N_HEADS, N_HEADS), 0) ==
               lax.broadcasted_iota(jnp.int32, (N_HEADS, N_HEADS), 1)).astype(F32)
        lp_t = lax.dot_general(eye, lpad[...], (((1,), (1,)), ((), ())),
                               precision=lax.Precision.HIGHEST, preferred_element_type=F32)
        bias, carry = suffix_bias(lp_t * LOG2E, jnp.zeros((N_HEADS, 128), F32))
        carry_ref[...] = carry
        s = lax.dot_general(qbd_ref[...], kpad[...].astype(BF16), (((1,), (1,)), ((), ())),
                            preferred_element_type=F32)
        s = s + jnp.concatenate([bias] * n_new, axis=0)
        q_idx = lax.broadcasted_iota(jnp.int32, (rows, PAGE), 0) // N_HEADS
        k_idx = lax.broadcasted_iota(jnp.int32, (rows, PAGE), 1)
        s = jnp.where(k_idx <= q_idx, s, NEG_BIG)
        vb = vpad[...].astype(BF16)
        accumulate(s, lambda p: jnp.dot(p, vb, preferred_element_type=F32))

    for c in copies(b, gi, slot):
        c.wait()

    carry = carry_ref[...]
    biases = [None] * PAGES_PER_STEP
    for p in reversed(range(PAGES_PER_STEP)):
        biases[p], carry = suffix_bias(lbuf[slot, p] * LOG2E, carry)
    carry_ref[...] = carry
    bias = jnp.concatenate(biases, axis=1)
    kcat = jnp.concatenate([kbuf[slot, p].astype(BF16) for p in range(PAGES_PER_STEP)], axis=1)
    s = jnp.dot(qbd_ref[...], kcat, preferred_element_type=F32)
    s = s + jnp.concatenate([bias] * n_new, axis=0)
    vcat = jnp.concatenate([vbuf[slot, p].astype(BF16) for p in range(PAGES_PER_STEP)], axis=1)
    accumulate(s, lambda p: lax.dot_general(p, vcat, (((1,), (1,)), ((), ())),
                                            preferred_element_type=F32))

    @pl.when(gi == groups - 1)
    def _():
        acc = acc_ref[...] * (1.0 / l_ref[...])
        for t in range(n_new):
            blk = jnp.where(own, acc[t * N_HEADS:(t + 1) * N_HEADS, :], 0.0)
            o_ref[0, t:t + 1, :] = jnp.sum(blk, axis=0, keepdims=True)


def _attn_sample(page_table, q, k_new, v_new, logf_new, cache_kt, cache_vt, cache_lt):
    n_seq, n_new, d = q.shape
    n_pages = page_table.shape[1]
    groups = n_pages // PAGES_PER_STEP
    rows = n_new * N_HEADS
    new_spec = pl.BlockSpec((1, n_new, d), lambda i, j, pt: (i, 0, 0))
    any_spec = pl.BlockSpec(memory_space=pl.ANY)
    grid_spec = pltpu.PrefetchScalarGridSpec(
        num_scalar_prefetch=1,
        grid=(n_seq, groups),
        in_specs=[new_spec, new_spec, new_spec,
                  pl.BlockSpec((1, n_new, N_HEADS), lambda i, j, pt: (i, 0, 0)),
                  any_spec, any_spec, any_spec],
        out_specs=new_spec,
        scratch_shapes=[
            pltpu.VMEM((2, PAGES_PER_STEP, d, PAGE), F32),
            pltpu.VMEM((2, PAGES_PER_STEP, d, PAGE), F32),
            pltpu.VMEM((2, PAGES_PER_STEP, N_HEADS, PAGE), F32),
            pltpu.SemaphoreType.DMA((3, 2)),
            pltpu.VMEM((rows, d), BF16),
            pltpu.VMEM((PAGE, d), F32),
            pltpu.VMEM((PAGE, d), F32),
            pltpu.VMEM((PAGE, N_HEADS), F32),
            pltpu.VMEM((rows, 1), F32),
            pltpu.VMEM((rows, 1), F32),
            pltpu.VMEM((rows, d), F32),
            pltpu.VMEM((N_HEADS, 128), F32),
        ],
    )
    return pl.pallas_call(
        functools.partial(_attn_sample_body, n_new, groups),
        grid_spec=grid_spec,
        out_shape=jax.ShapeDtypeStruct((n_seq, n_new, d), F32),
        compiler_params=pltpu.CompilerParams(
            dimension_semantics=("arbitrary", "arbitrary"),
            vmem_limit_bytes=VMEM_LIMIT),
        name="attn_sample",
    )(page_table, q, k_new, v_new, logf_new, cache_kt, cache_vt, cache_lt)


def _out_body(o_ref, gate_ref, x1_ref, w_ref, g_ref, y_ref):
    og = (o_ref[...].astype(F32) * gate_ref[...].astype(F32)).astype(BF16)
    x2 = x1_ref[...] + jnp.dot(og, w_ref[...], preferred_element_type=F32)
    y_ref[...] = _rms_norm(x2, g_ref[...])


def _out_proj(o, gate, x1, w, g):
    rows, d = x1.shape
    tile = min(OUT_TILE, rows)
    row_spec = pl.BlockSpec((tile, d), lambda i: (i, 0))
    return pl.pallas_call(
        _out_body,
        grid=(rows // tile,),
        in_specs=[row_spec, row_spec, row_spec, _const_spec(w.shape), _const_spec((1, d))],
        out_specs=row_spec,
        out_shape=jax.ShapeDtypeStruct((rows, d), F32),
        compiler_params=pltpu.CompilerParams(
            dimension_semantics=("parallel",), vmem_limit_bytes=VMEM_LIMIT),
        name="out_proj",
    )(o, gate, x1, w, g)


def kernel(x_prompt, x_sample, state_pool, cache_k, cache_v, cache_logf, page_table,
           norm_g, w_in_pool, pool_mix, pool_scale, w_out_pool,
           w_in_attn, b_forget, w_out_attn, final_norm):
    b, t, d = x_prompt.shape
    n_seq, n_new, _ = x_sample.shape
    n_phys = cache_k.shape[1]
    past = page_table.shape[1] * cache_k.shape[2]
    assert norm_g.shape[0] == 2 and d == D_MODEL and cache_k.shape[2] == PAGE

    g0 = norm_g[0].reshape(1, d)
    g1 = norm_g[1].reshape(1, d)
    gf = final_norm.reshape(1, d)
    win = w_in_pool[0].astype(BF16)
    mix = pool_mix[0].astype(BF16)
    scale = pool_scale[0].reshape(1, POOL_WIDTH)
    wout_pool = w_out_pool[0].astype(BF16)
    w_attn = w_in_attn[0, :, :4 * d].astype(BF16)
    wf = jnp.pad(w_in_attn[0, :, 4 * d:], ((0, 0), (0, 128 - N_HEADS))).astype(BF16)
    bf = jnp.pad(b_forget[0], (0, 128 - N_HEADS)).reshape(1, 128)
    wout_attn = w_out_attn[0].astype(BF16)

    x1p, last = _pool_prompt(x_prompt, g0, win, mix, scale, wout_pool)
    qp, kp, vp, k32, v32, gate_p, logf_p, bias_p = _proj_prompt(x1p, g1, w_attn, wf, bf)
    o_p = _attn_prompt(qp, kp, vp, bias_p)
    y_p = _out_proj(o_p.reshape(b * t, d), gate_p.reshape(b * t, d), x1p.reshape(b * t, d),
                    wout_attn, gf).reshape(b, t, d)

    xs_tm = x_sample.transpose(1, 0, 2).reshape(n_new * n_seq, d)
    st_tm = state_pool[0].transpose(1, 0, 2).reshape(POOL_CTX * n_seq, POOL_WIDTH)
    x1s_tm, u_tm = _pool_sample(xs_tm, st_tm, n_seq, n_new, past, g0, win, mix, scale, wout_pool)
    x1s = x1s_tm.reshape(n_new, n_seq, d).transpose(1, 0, 2).reshape(n_seq * n_new, d)
    u_new = u_tm.reshape(n_new, n_seq, POOL_WIDTH).transpose(1, 0, 2)
    qs, ks, vs, gate_s, logf_s = _proj_sample(x1s, g1, w_attn, wf, bf)
    o_s = _attn_sample(page_table,
                       qs.reshape(n_seq, n_new, d), ks.reshape(n_seq, n_new, d),
                       vs.reshape(n_seq, n_new, d), logf_s.reshape(n_seq, n_new, N_HEADS),
                       cache_k[0].transpose(0, 2, 3, 1).reshape(n_phys, d, PAGE),
                       cache_v[0].transpose(0, 2, 3, 1).reshape(n_phys, d, PAGE),
                       cache_logf[0].transpose(0, 2, 1))
    y_s = _out_proj(o_s.reshape(n_seq * n_new, d).astype(BF16), gate_s, x1s, wout_attn, gf)

    pool_prompt = last[:, POOL_HALO - POOL_CTX:, :][None]
    pool_sample = jnp.concatenate([state_pool[0][:, n_new:, :], u_new], axis=1)[None]
    heads = (N_HEADS, HEAD_DIM)
    return (y_p, y_s.reshape(n_seq, n_new, d), pool_prompt, pool_sample,
            k32.reshape(1, b, t, *heads), v32.reshape(1, b, t, *heads), logf_p[None],
            ks.reshape(1, n_seq, n_new, *heads), vs.reshape(1, n_seq, n_new, *heads),
            logf_s.reshape(1, n_seq, n_new, N_HEADS))
```

```python
import functools

import jax
import jax.numpy as jnp
from jax import lax
from jax.experimental import pallas as pl
from jax.experimental.pallas import tpu as pltpu

F32 = jnp.float32
BF16 = jnp.bfloat16

D_MODEL = 1024
POOL_WIDTH = 2048
POOL_WINDOWS = (2, 4, 8, 16)
POOL_GROUP = POOL_WIDTH // len(POOL_WINDOWS)
POOL_CTX = max(POOL_WINDOWS) - 1
POOL_HALO = 16
N_HEADS = 16
HEAD_DIM = 64
HEAD_PAIRS = N_HEADS // 2
PAIR_WIDTH = 2 * HEAD_DIM
RMS_EPS = 1e-6
NEG_BIG = -1e30
LOG2E = 1.4426950408889634

ROW_TILE = 512
OUT_TILE = 1024
Q_TILE = 1024
KV_TILE = 512
PAGE = 128
PAGES_PER_STEP = 8
SAMPLE_STEPS = 2

VMEM_LIMIT = 56 * 1024 * 1024


def _rms_norm(x, g):
    ms = jnp.mean(x * x, axis=-1, keepdims=True)
    return x * lax.rsqrt(ms + RMS_EPS) * g


def _silu(z):
    return z * jax.nn.sigmoid(z)


def _log_sigmoid(x):
    return jnp.minimum(x, 0.0) - jnp.log1p(jnp.exp(-jnp.abs(x)))


def _const_spec(shape):
    return pl.BlockSpec(shape, lambda *_: (0,) * len(shape), pipeline_mode=pl.Buffered(1))


def _pool_tail(hb, pooled_fn, win_ref, mix_ref, scale_ref, wout_ref, resid):
    acc = resid
    for g in range(len(POOL_WINDOWS)):
        cols = slice(g * POOL_GROUP, (g + 1) * POOL_GROUP)
        zcols = slice(POOL_WIDTH + g * POOL_GROUP, POOL_WIDTH + (g + 1) * POOL_GROUP)
        u = jnp.dot(hb, win_ref[:, cols], preferred_element_type=F32)
        pooled = pooled_fn(g, u)
        y = jnp.dot(pooled.astype(BF16), mix_ref[g], preferred_element_type=F32)
        y = y * scale_ref[:, cols]
        z = jnp.dot(hb, win_ref[:, zcols], preferred_element_type=F32)
        yg = (y * _silu(z)).astype(BF16)
        acc = acc + jnp.dot(yg, wout_ref[cols, :], preferred_element_type=F32)
    return acc


def _pool_prompt_body(x_ref, g_ref, win_ref, mix_ref, scale_ref, wout_ref,
                      x1_ref, last_ref, halo_ref):
    t = pl.program_id(1)

    @pl.when(t == 0)
    def _():
        halo_ref[...] = jnp.zeros_like(halo_ref)

    x = x_ref[0]
    hb = _rms_norm(x, g_ref[...]).astype(BF16)
    pos = t * ROW_TILE + lax.broadcasted_iota(jnp.int32, (ROW_TILE, 1), 0)

    def pooled_fn(g, u):
        w = POOL_WINDOWS[g]
        cols = slice(g * POOL_GROUP, (g + 1) * POOL_GROUP)
        e = jnp.concatenate([halo_ref[:, cols], u], axis=0)
        tail = u[ROW_TILE - POOL_HALO:, :]
        halo_ref[:, cols] = tail
        last_ref[0, :, cols] = tail
        sh = 1
        while sh < w:
            e = e + pltpu.roll(e, sh, 0)
            sh *= 2
        inv = 1.0 / jnp.minimum(pos + 1, w).astype(F32)
        return e[POOL_HALO:, :] * inv - u

    x1_ref[0] = _pool_tail(hb, pooled_fn, win_ref, mix_ref, scale_ref, wout_ref, x)


def _pool_prompt(x, g, win, mix, scale, wout):
    b, t, d = x.shape
    grid = (b, t // ROW_TILE)
    return pl.pallas_call(
        _pool_prompt_body,
        grid=grid,
        in_specs=[
            pl.BlockSpec((1, ROW_TILE, d), lambda i, j: (i, j, 0)),
            _const_spec((1, d)),
            _const_spec(win.shape),
            _const_spec(mix.shape),
            _const_spec((1, POOL_WIDTH)),
            _const_spec(wout.shape),
        ],
        out_specs=[
            pl.BlockSpec((1, ROW_TILE, d), lambda i, j: (i, j, 0)),
            pl.BlockSpec((1, POOL_HALO, POOL_WIDTH), lambda i, j: (i, 0, 0)),
        ],
        out_shape=[
            jax.ShapeDtypeStruct((b, t, d), F32),
            jax.ShapeDtypeStruct((b, POOL_HALO, POOL_WIDTH), F32),
        ],
        scratch_shapes=[pltpu.VMEM((POOL_HALO, POOL_WIDTH), F32)],
        compiler_params=pltpu.CompilerParams(
            dimension_semantics=("parallel", "arbitrary"),
            vmem_limit_bytes=VMEM_LIMIT),
        name="pool_prompt",
    )(x, g, win, mix, scale, wout)


def _pool_sample_body(n_seq, n_new, start, x_ref, st_ref, g_ref, win_ref, mix_ref,
                      scale_ref, wout_ref, x1_ref, u_ref):
    x = x_ref[...]
    hb = _rms_norm(x, g_ref[...]).astype(BF16)

    def pooled_fn(g, u):
        w = POOL_WINDOWS[g]
        cols = slice(g * POOL_GROUP, (g + 1) * POOL_GROUP)
        u_ref[:, cols] = u

        def slab(e):
            if e < POOL_CTX:
                return st_ref[e * n_seq:(e + 1) * n_seq, cols]
            return u[(e - POOL_CTX) * n_seq:(e - POOL_CTX + 1) * n_seq, :]

        outs = []
        for t in range(n_new):
            s = slab(POOL_CTX + t)
            for e in range(POOL_CTX + t - w + 1, POOL_CTX + t):
                s = s + slab(e)
            cnt = min(start + t + 1, w)
            outs.append(s * (1.0 / cnt) - slab(POOL_CTX + t))
        return jnp.concatenate(outs, axis=0)

    x1_ref[...] = _pool_tail(hb, pooled_fn, win_ref, mix_ref, scale_ref, wout_ref, x)


def _pool_sample(x_tm, state_tm, n_seq, n_new, start, g, win, mix, scale, wout):
    rows, d = x_tm.shape
    return pl.pallas_call(
        functools.partial(_pool_sample_body, n_seq, n_new, start),
        out_shape=[
            jax.ShapeDtypeStruct((rows, d), F32),
            jax.ShapeDtypeStruct((rows, POOL_WIDTH), F32),
        ],
        compiler_params=pltpu.CompilerParams(vmem_limit_bytes=VMEM_LIMIT),
        name="pool_sample",
    )(x_tm, state_tm, g, win, mix, scale, wout)


def _project(x, g_ref, w_ref, wf_ref, bf_ref):
    hb = _rms_norm(x, g_ref[...]).astype(BF16)
    parts = [jnp.dot(hb, w_ref[:, i * D_MODEL:(i + 1) * D_MODEL], preferred_element_type=F32)
             for i in range(4)]
    f = jnp.dot(hb, wf_ref[...], preferred_element_type=F32) + bf_ref[...]
    q = parts[0] * (LOG2E * HEAD_DIM ** -0.5)
    return q, parts[1], parts[2], _silu(parts[3]), _log_sigmoid(f)


def _lane_cumsum(x):
    lane = lax.broadcasted_iota(jnp.int32, x.shape, 1)
    sh = 1
    while sh < x.shape[1]:
        x = x + jnp.where(lane >= sh, pltpu.roll(x, sh, 1), 0.0)
        sh *= 2
    return x


def _proj_prompt_body(x_ref, g_ref, w_ref, wf_ref, bf_ref,
                      q_ref, kp_ref, vp_ref, k_ref, v_ref, gate_ref, logf_ref, bias_ref,
                      carry_ref):
    t = pl.program_id(1)

    @pl.when(t == 0)
    def _():
        carry_ref[...] = jnp.zeros_like(carry_ref)

    q, k, v, gate, logf = _project(x_ref[0], g_ref, w_ref, wf_ref, bf_ref)
    k_ref[0] = k
    v_ref[0] = v
    gate_ref[0] = gate.astype(BF16)
    logf_ref[0] = logf[:, :N_HEADS]
    qb, kb, vb = q.astype(BF16), k.astype(BF16), v.astype(BF16)
    for p in range(HEAD_PAIRS):
        cols = slice(p * PAIR_WIDTH, (p + 1) * PAIR_WIDTH)
        q_ref[0, p] = qb[:, cols]
        kp_ref[0, p] = kb[:, cols]
        vp_ref[0, p] = vb[:, cols]

    lt = logf.T[:N_HEADS, :]
    carry = carry_ref[...]
    for c in range(ROW_TILE // 128):
        csum = _lane_cumsum(lt[:, c * 128:(c + 1) * 128]) + carry
        neg = csum * (-LOG2E)
        for p in range(HEAD_PAIRS):
            bias_ref[0, p, 0, :, c * 128:(c + 1) * 128] = neg[2 * p:2 * p + 2, :]
        carry = jnp.broadcast_to(csum[:, 127:128], carry.shape)
    carry_ref[...] = carry


def _proj_prompt(x1, g, w, wf, bf):
    b, t, d = x1.shape
    grid = (b, t // ROW_TILE)
    per_kv = KV_TILE // ROW_TILE
    row_spec = pl.BlockSpec((1, ROW_TILE, d), lambda i, j: (i, j, 0))
    pair_spec = pl.BlockSpec((1, HEAD_PAIRS, ROW_TILE, PAIR_WIDTH), lambda i, j: (i, 0, j, 0))
    pair_shape = jax.ShapeDtypeStruct((b, HEAD_PAIRS, t, PAIR_WIDTH), BF16)
    return pl.pallas_call(
        _proj_prompt_body,
        grid=grid,
        in_specs=[row_spec, _const_spec((1, d)), _const_spec(w.shape),
                  _const_spec(wf.shape), _const_spec(bf.shape)],
        out_specs=[
            pair_spec, pair_spec, pair_spec, row_spec, row_spec, row_spec,
            pl.BlockSpec((1, ROW_TILE, N_HEADS), lambda i, j: (i, j, 0)),
            pl.BlockSpec((1, HEAD_PAIRS, 1, 2, ROW_TILE),
                         lambda i, j: (i, 0, j // per_kv, 0, j % per_kv)),
        ],
        out_shape=[
            pair_shape, pair_shape, pair_shape,
            jax.ShapeDtypeStruct((b, t, d), F32),
            jax.ShapeDtypeStruct((b, t, d), F32),
            jax.ShapeDtypeStruct((b, t, d), BF16),
            jax.ShapeDtypeStruct((b, t, N_HEADS), F32),
            jax.ShapeDtypeStruct((b, HEAD_PAIRS, t // KV_TILE, 2, KV_TILE), F32),
        ],
        scratch_shapes=[pltpu.VMEM((N_HEADS, 128), F32)],
        compiler_params=pltpu.CompilerParams(
            dimension_semantics=("parallel", "arbitrary"),
            vmem_limit_bytes=VMEM_LIMIT),
        name="proj_prompt",
    )(x1, g, w, wf, bf)


def _proj_sample_body(x_ref, g_ref, w_ref, wf_ref, bf_ref,
                      q_ref, k_ref, v_ref, gate_ref, logf_ref):
    q, k, v, gate, logf = _project(x_ref[...], g_ref, w_ref, wf_ref, bf_ref)
    q_ref[...] = q.astype(BF16)
    k_ref[...] = k
    v_ref[...] = v
    gate_ref[...] = gate.astype(BF16)
    logf_ref[...] = logf[:, :N_HEADS]


def _proj_sample(x1, g, w, wf, bf):
    rows, d = x1.shape
    return pl.pallas_call(
        _proj_sample_body,
        out_shape=[
            jax.ShapeDtypeStruct((rows, d), BF16),
            jax.ShapeDtypeStruct((rows, d), F32),
            jax.ShapeDtypeStruct((rows, d), F32),
            jax.ShapeDtypeStruct((rows, d), BF16),
            jax.ShapeDtypeStruct((rows, N_HEADS), F32),
        ],
        compiler_params=pltpu.CompilerParams(vmem_limit_bytes=VMEM_LIMIT),
        name="proj_sample",
    )(x1, g, w, wf, bf)


def _attn_body(n_new, groups, n_sample_steps, pt_ref,
               q_ref, k_ref, v_ref, bias_ref,
               qs_ref, kn_ref, vn_ref, ln_ref, ck_hbm, cv_hbm, cl_hbm,
               o_ref, os_ref,
               kbuf, vbuf, lbuf, sems, qbd_ref, kpad, vpad, lpad,
               m_ref, l_ref, acc_ref, carry_ref):
    qi = pl.program_id(2)
    flat = (pl.program_id(0) * pl.num_programs(1) + pl.program_id(1)) * pl.num_programs(2) + qi
    rows = n_new * N_HEADS

    def copies(sidx, sl):
        bb = sidx // groups
        gg = sidx % groups
        out = []
        for p in range(PAGES_PER_STEP):
            page = pt_ref[bb, (groups - 1 - gg) * PAGES_PER_STEP + p]
            out.append(pltpu.make_async_copy(ck_hbm.at[page], kbuf.at[sl, p], sems.at[0, sl]))
            out.append(pltpu.make_async_copy(cv_hbm.at[page], vbuf.at[sl, p], sems.at[1, sl]))
            out.append(pltpu.make_async_copy(cl_hbm.at[page], lbuf.at[sl, p], sems.at[2, sl]))
        return out

    row_head = lax.broadcasted_iota(jnp.int32, (N_HEADS, D_MODEL), 0)
    lane_head = lax.broadcasted_iota(jnp.int32, (N_HEADS, D_MODEL), 1) // HEAD_DIM
    own = row_head == lane_head

    def suffix_bias(lp_t, carry):
        incl = _lane_cumsum(lp_t)
        total = jnp.broadcast_to(incl[:, 127:128], incl.shape)
        return total - incl + carry, carry + total

    def accumulate(s, pv_fn):
        m_old = m_ref[...]
        m_new = jnp.maximum(m_old, jnp.max(s, axis=1, keepdims=True))
        alpha = jnp.exp2(m_old - m_new)
        p = jnp.exp2(s - m_new)
        l_ref[...] = alpha * l_ref[...] + jnp.sum(p, axis=1, keepdims=True)
        acc_ref[...] = acc_ref[...] * alpha + pv_fn(p.astype(BF16))
        m_ref[...] = m_new

    def sample_begin():
        q = qs_ref[0].astype(F32)
        for t in range(n_new):
            qt = jnp.broadcast_to(q[t:t + 1, :], (N_HEADS, D_MODEL))
            qbd_ref[t * N_HEADS:(t + 1) * N_HEADS, :] = jnp.where(own, qt, 0.0).astype(BF16)
        m_ref[...] = jnp.full_like(m_ref, NEG_BIG)
        l_ref[...] = jnp.zeros_like(l_ref)
        acc_ref[...] = jnp.zeros_like(acc_ref)
        kpad[0:n_new, :] = kn_ref[0]
        vpad[0:n_new, :] = vn_ref[0]
        lpad[0:n_new, :] = ln_ref[0]
        eye = (lax.broadcasted_iota(jnp.int32, (N_HEADS, N_HEADS), 0) ==
               lax.broadcasted_iota(jnp.int32, (N_HEADS, N_HEADS), 1)).astype(F32)
        lp_t = lax.dot_general(eye, lpad[...], (((1,), (1,)), ((), ())),
                               precision=lax.Precision.HIGHEST, preferred_element_type=F32)
        bias, carry = suffix_bias(lp_t * LOG2E, jnp.zeros((N_HEADS, 128), F32))
        carry_ref[...] = carry
        s = lax.dot_general(qbd_ref[...], kpad[...].astype(BF16), (((1,), (1,)), ((), ())),
                            preferred_element_type=F32)
        s = s + jnp.concatenate([bias] * n_new, axis=0)
        q_idx = lax.broadcasted_iota(jnp.int32, (rows, PAGE), 0) // N_HEADS
        k_idx = lax.broadcasted_iota(jnp.int32, (rows, PAGE), 1)
        s = jnp.where(k_idx <= q_idx, s, NEG_BIG)
        vb = vpad[...].astype(BF16)
        accumulate(s, lambda p: jnp.dot(p, vb, preferred_element_type=F32))

    def sample_pages(slot):
        carry = carry_ref[...]
        biases = [None] * PAGES_PER_STEP
        for p in reversed(range(PAGES_PER_STEP)):
            biases[p], carry = suffix_bias(lbuf[slot, p] * LOG2E, carry)
        carry_ref[...] = carry
        bias = jnp.concatenate(biases, axis=1)
        kcat = jnp.concatenate([kbuf[slot, p].astype(BF16) for p in range(PAGES_PER_STEP)], axis=1)
        s = jnp.dot(qbd_ref[...], kcat, preferred_element_type=F32)
        s = s + jnp.concatenate([bias] * n_new, axis=0)
        vcat = jnp.concatenate([vbuf[slot, p].astype(BF16) for p in range(PAGES_PER_STEP)], axis=1)
        accumulate(s, lambda p: lax.dot_general(p, vcat, (((1,), (1,)), ((), ())),
                                                preferred_element_type=F32))

    def sample_end():
        acc = acc_ref[...] * (1.0 / l_ref[...])
        for t in range(n_new):
            blk = jnp.where(own, acc[t * N_HEADS:(t + 1) * N_HEADS, :], 0.0)
            os_ref[0, t:t + 1, :] = jnp.sum(blk, axis=0, keepdims=True)

    @pl.when(flat == 0)
    def _():
        for c in copies(0, 0):
            c.start()
        kpad[...] = jnp.zeros_like(kpad)
        vpad[...] = jnp.zeros_like(vpad)
        lpad[...] = jnp.zeros_like(lpad)

    def sample_step(u):
        sidx = flat * SAMPLE_STEPS + u
        slot = u % 2
        gi = sidx % groups

        @pl.when(sidx + 1 < n_sample_steps)
        def _():
            for c in copies(sidx + 1, 1 - slot):
                c.start()

        pl.when(gi == 0)(sample_begin)
        for c in copies(sidx, slot):
            c.wait()
        sample_pages(slot)
        pl.when(gi == groups - 1)(sample_end)

    q = q_ref[0, 0]
    lane = lax.broadcasted_iota(jnp.int32, (1, PAIR_WIDTH), 1)
    first = lane < HEAD_DIM
    zero = jnp.zeros_like(q)
    q_heads = (jnp.where(first, q, zero), jnp.where(first, zero, q))
    half = Q_TILE // 2
    causal = (lax.broadcasted_iota(jnp.int32, (half, KV_TILE), 0) >=
              lax.broadcasted_iota(jnp.int32, (half, KV_TILE), 1))

    def head_update(qh, kt, vt, brow, m, l, mask_rows):
        s = lax.dot_general(qh, kt, (((1,), (1,)), ((), ())), preferred_element_type=F32)
        s = s + brow
        if mask_rows is not None:
            top = jnp.where(causal, s[:mask_rows], NEG_BIG)
            s = top if mask_rows == s.shape[0] else jnp.concatenate([top, s[mask_rows:]], axis=0)
        m_new = jnp.maximum(m, jnp.max(s, axis=1, keepdims=True))
        alpha = jnp.exp2(m - m_new)
        p = jnp.exp2(s - m_new)
        l_new = alpha * l + jnp.sum(p, axis=1, keepdims=True)
        pv = jnp.dot(p.astype(BF16), vt, preferred_element_type=F32)
        return m_new, l_new, alpha, pv

    def key_tile(t, carry, rows_, mask_rows):
        m0, l0, m1, l1, acc = carry
        start = pl.multiple_of(t * KV_TILE, KV_TILE)
        kt = k_ref[0, 0, pl.ds(start, KV_TILE), :]
        vt = v_ref[0, 0, pl.ds(start, KV_TILE), :]
        bt = bias_ref[0, 0, t]
        m0, l0, a0, pv0 = head_update(q_heads[0][rows_], kt, vt, bt[0:1, :], m0, l0, mask_rows)
        m1, l1, a1, pv1 = head_update(q_heads[1][rows_], kt, vt, bt[1:2, :], m1, l1, mask_rows)
        acc = acc * jnp.where(first, a0, a1) + jnp.where(first, pv0, pv1)
        return m0, l0, m1, l1, acc

    everything = slice(0, Q_TILE)

    def visible_pair(i, carry):
        carry = key_tile(2 * i, carry, everything, None)
        return key_tile(2 * i + 1, carry, everything, None)

    neg = jnp.full((Q_TILE, 1), NEG_BIG, F32)
    zcol = jnp.zeros((Q_TILE, 1), F32)
    init = (neg, zcol, neg, zcol, jnp.zeros((Q_TILE, PAIR_WIDTH), F32))
    carry = lax.fori_loop(0, qi, visible_pair, init)
    for u in range(SAMPLE_STEPS):
        sample_step(u)
    m0, l0, m1, l1, acc = key_tile(2 * qi, carry, everything, half)
    lower = slice(half, Q_TILE)
    low = key_tile(2 * qi + 1, (m0[lower], l0[lower], m1[lower], l1[lower], acc[lower]),
                   lower, half)
    o_ref[0, :half, :] = (acc[:half] * jnp.where(first, 1.0 / l0[:half], 1.0 / l1[:half])
                          ).astype(BF16)
    o_ref[0, half:, :] = (low[4] * jnp.where(first, 1.0 / low[1], 1.0 / low[3])).astype(BF16)


def _attention(q, k, v, bias, page_table, qs, k_new, v_new, logf_new,
               cache_kt, cache_vt, cache_lt):
    b, hp, t, w = q.shape
    n_seq, n_new, d = qs.shape
    groups = page_table.shape[1] // PAGES_PER_STEP
    rows = n_new * N_HEADS
    grid = (b, hp, t // Q_TILE)
    n_grid = b * hp * (t // Q_TILE)
    n_sample_steps = n_seq * groups
    assert Q_TILE == 2 * KV_TILE and KV_TILE == ROW_TILE
    assert n_sample_steps == n_grid * SAMPLE_STEPS and groups % SAMPLE_STEPS == 0

    def seq_of(i, p, j):
        return (((i * hp + p) * (t // Q_TILE) + j) * SAMPLE_STEPS) // groups

    new_spec = pl.BlockSpec((1, n_new, d), lambda i, p, j, pt: (seq_of(i, p, j), 0, 0))
    any_spec = pl.BlockSpec(memory_space=pl.ANY)
    grid_spec = pltpu.PrefetchScalarGridSpec(
        num_scalar_prefetch=1,
        grid=grid,
        in_specs=[
            pl.BlockSpec((1, 1, Q_TILE, w), lambda i, p, j, pt: (i, p, j, 0)),
            pl.BlockSpec((1, 1, t, w), lambda i, p, j, pt: (i, p, 0, 0)),
            pl.BlockSpec((1, 1, t, w), lambda i, p, j, pt: (i, p, 0, 0)),
            pl.BlockSpec((1, 1, t // KV_TILE, 2, KV_TILE), lambda i, p, j, pt: (i, p, 0, 0, 0)),
            new_spec, new_spec, new_spec,
            pl.BlockSpec((1, n_new, N_HEADS), lambda i, p, j, pt: (seq_of(i, p, j), 0, 0)),
            any_spec, any_spec, any_spec,
        ],
        out_specs=[
            pl.BlockSpec((1, Q_TILE, w), lambda i, p, j, pt: (i, j, p)),
            new_spec,
        ],
        scratch_shapes=[
            pltpu.VMEM((2, PAGES_PER_STEP, d, PAGE), F32),
            pltpu.VMEM((2, PAGES_PER_STEP, d, PAGE), F32),
            pltpu.VMEM((2, PAGES_PER_STEP, N_HEADS, PAGE), F32),
            pltpu.SemaphoreType.DMA((3, 2)),
            pltpu.VMEM((rows, d), BF16),
            pltpu.VMEM((PAGE, d), F32),
            pltpu.VMEM((PAGE, d), F32),
            pltpu.VMEM((PAGE, N_HEADS), F32),
            pltpu.VMEM((rows, 1), F32),
            pltpu.VMEM((rows, 1), F32),
            pltpu.VMEM((rows, d), F32),
            pltpu.VMEM((N_HEADS, 128), F32),
        ],
    )
    return pl.pallas_call(
        functools.partial(_attn_body, n_new, groups, n_sample_steps),
        grid_spec=grid_spec,
        out_shape=[jax.ShapeDtypeStruct((b, t, hp * w), BF16),
                   jax.ShapeDtypeStruct((n_seq, n_new, d), F32)],
        compiler_params=pltpu.CompilerParams(
            dimension_semantics=("arbitrary", "arbitrary", "arbitrary"),
            vmem_limit_bytes=VMEM_LIMIT),
        name="attention",
    )(page_table, q, k, v, bias, qs, k_new, v_new, logf_new, cache_kt, cache_vt, cache_lt)


def _out_body(o_ref, gate_ref, x1_ref, w_ref, g_ref, y_ref):
    og = (o_ref[...].astype(F32) * gate_ref[...].astype(F32)).astype(BF16)
    x2 = x1_ref[...] + jnp.dot(og, w_ref[...], preferred_element_type=F32)
    y_ref[...] = _rms_norm(x2, g_ref[...])


def _out_proj(o, gate, x1, w, g):
    rows, d = x1.shape
    tile = min(OUT_TILE, rows)
    row_spec = pl.BlockSpec((tile, d), lambda i: (i, 0))
    return pl.pallas_call(
        _out_body,
        grid=(rows // tile,),
        in_specs=[row_spec, row_spec, row_spec, _const_spec(w.shape), _const_spec((1, d))],
        out_specs=row_spec,
        out_shape=jax.ShapeDtypeStruct((rows, d), F32),
        compiler_params=pltpu.CompilerParams(
            dimension_semantics=("parallel",), vmem_limit_bytes=VMEM_LIMIT),
        name="out_proj",
    )(o, gate, x1, w, g)


def kernel(x_prompt, x_sample, state_pool, cache_k, cache_v, cache_logf, page_table,
           norm_g, w_in_pool, pool_mix, pool_scale, w_out_pool,
           w_in_attn, b_forget, w_out_attn, final_norm):
    b, t, d = x_prompt.shape
    n_seq, n_new, _ = x_sample.shape
    n_phys = cache_k.shape[1]
    past = page_table.shape[1] * cache_k.shape[2]
    assert norm_g.shape[0] == 2 and d == D_MODEL and cache_k.shape[2] == PAGE

    g0 = norm_g[0].reshape(1, d)
    g1 = norm_g[1].reshape(1, d)
    gf = final_norm.reshape(1, d)
    win = w_in_pool[0].astype(BF16)
    mix = pool_mix[0].astype(BF16)
    scale = pool_scale[0].reshape(1, POOL_WIDTH)
    wout_pool = w_out_pool[0].astype(BF16)
    w_attn = w_in_attn[0, :, :4 * d].astype(BF16)
    wf = jnp.pad(w_in_attn[0, :, 4 * d:], ((0, 0), (0, 128 - N_HEADS))).astype(BF16)
    bf = jnp.pad(b_forget[0], (0, 128 - N_HEADS)).reshape(1, 128)
    wout_attn = w_out_attn[0].astype(BF16)

    x1p, last = _pool_prompt(x_prompt, g0, win, mix, scale, wout_pool)
    xs_tm = x_sample.transpose(1, 0, 2).reshape(n_new * n_seq, d)
    st_tm = state_pool[0].transpose(1, 0, 2).reshape(POOL_CTX * n_seq, POOL_WIDTH)
    x1s_tm, u_tm = _pool_sample(xs_tm, st_tm, n_seq, n_new, past, g0, win, mix, scale, wout_pool)
    x1s = x1s_tm.reshape(n_new, n_seq, d).transpose(1, 0, 2).reshape(n_seq * n_new, d)
    u_new = u_tm.reshape(n_new, n_seq, POOL_WIDTH).transpose(1, 0, 2)

    qp, kp, vp, k32, v32, gate_p, logf_p, bias_p = _proj_prompt(x1p, g1, w_attn, wf, bf)
    qs, ks, vs, gate_s, logf_s = _proj_sample(x1s, g1, w_attn, wf, bf)
    o_p, o_s = _attention(qp, kp, vp, bias_p, page_table,
                          qs.reshape(n_seq, n_new, d), ks.reshape(n_seq, n_new, d),
                          vs.reshape(n_seq, n_new, d), logf_s.reshape(n_seq, n_new, N_HEADS),
                          cache_k[0].transpose(0, 2, 3, 1).reshape(n_phys, d, PAGE),
                          cache_v[0].transpose(0, 2, 3, 1).reshape(n_phys, d, PAGE),
                          cache_logf[0].transpose(0, 2, 1))
    y_p = _out_proj(o_p.reshape(b * t, d), gate_p.reshape(b * t, d), x1p.reshape(b * t, d),
                    wout_attn, gf).reshape(b, t, d)
    y_s = _out_proj(o_s.reshape(n_seq * n_new, d).astype(BF16), gate_s, x1s, wout_attn, gf)

    pool_prompt = last[:, POOL_HALO - POOL_CTX:, :][None]
    pool_sample = jnp.concatenate([state_pool[0][:, n_new:, :], u_new], axis=1)[None]
    heads = (N_HEADS, HEAD_DIM)
    return (y_p, y_s.reshape(n_seq, n_new, d), pool_prompt, pool_sample,
            k32.reshape(1, b, t, *heads), v32.reshape(1, b, t, *heads), logf_p[None],
            ks.reshape(1, n_seq, n_new, *heads), vs.reshape(1, n_seq, n_new, *heads),
            logf_s.reshape(1, n_seq, n_new, N_HEADS))
```

```python
import functools

import jax
import jax.numpy as jnp
from jax import lax
from jax.experimental import pallas as pl
from jax.experimental.pallas import tpu as pltpu

F32 = jnp.float32
BF16 = jnp.bfloat16

D_MODEL = 1024
POOL_WIDTH = 2048
POOL_WINDOWS = (2, 4, 8, 16)
POOL_GROUP = POOL_WIDTH // len(POOL_WINDOWS)
POOL_CTX = max(POOL_WINDOWS) - 1
POOL_HALO = 16
N_HEADS = 16
HEAD_DIM = 64
HEAD_PAIRS = N_HEADS // 2
PAIR_WIDTH = 2 * HEAD_DIM
RMS_EPS = 1e-6
NEG_BIG = -1e30
LOG2E = 1.4426950408889634

ROW_TILE = 512
OUT_TILE = 1024
Q_TILE = 1024
KV_TILE = 512
PAGE = 128
PAGES_PER_STEP = 8
SAMPLE_STEPS = 2

VMEM_LIMIT = 56 * 1024 * 1024


def _rms_norm(x, g):
    ms = jnp.mean(x * x, axis=-1, keepdims=True)
    return x * lax.rsqrt(ms + RMS_EPS) * g


def _silu(z):
    return z * jax.nn.sigmoid(z)


def _log_sigmoid(x):
    return jnp.minimum(x, 0.0) - jnp.log1p(jnp.exp(-jnp.abs(x)))


def _const_spec(shape):
    return pl.BlockSpec(shape, lambda *_: (0,) * len(shape), pipeline_mode=pl.Buffered(1))


def _pool_tail(hb, pooled_fn, win_ref, mix_ref, scale_ref, wout_ref, resid):
    acc = resid
    for g in range(len(POOL_WINDOWS)):
        cols = slice(g * POOL_GROUP, (g + 1) * POOL_GROUP)
        zcols = slice(POOL_WIDTH + g * POOL_GROUP, POOL_WIDTH + (g + 1) * POOL_GROUP)
        u = jnp.dot(hb, win_ref[:, cols], preferred_element_type=F32)
        pooled = pooled_fn(g, u)
        y = jnp.dot(pooled.astype(BF16), mix_ref[g], preferred_element_type=F32)
        y = y * scale_ref[:, cols]
        z = jnp.dot(hb, win_ref[:, zcols], preferred_element_type=F32)
        yg = (y * _silu(z)).astype(BF16)
        acc = acc + jnp.dot(yg, wout_ref[cols, :], preferred_element_type=F32)
    return acc


def _pool_prompt_body(x_ref, g_ref, win_ref, mix_ref, scale_ref, wout_ref,
                      x1_ref, last_ref, halo_ref):
    t = pl.program_id(1)

    @pl.when(t == 0)
    def _():
        halo_ref[...] = jnp.zeros_like(halo_ref)

    x = x_ref[0]
    hb = _rms_norm(x, g_ref[...]).astype(BF16)
    pos = t * ROW_TILE + lax.broadcasted_iota(jnp.int32, (ROW_TILE, 1), 0)

    def pooled_fn(g, u):
        w = POOL_WINDOWS[g]
        cols = slice(g * POOL_GROUP, (g + 1) * POOL_GROUP)
        e = jnp.concatenate([halo_ref[:, cols], u], axis=0)
        tail = u[ROW_TILE - POOL_HALO:, :]
        halo_ref[:, cols] = tail
        last_ref[0, :, cols] = tail
        sh = 1
        while sh < w:
            e = e + pltpu.roll(e, sh, 0)
            sh *= 2
        inv = 1.0 / jnp.minimum(pos + 1, w).astype(F32)
        return e[POOL_HALO:, :] * inv - u

    x1_ref[0] = _pool_tail(hb, pooled_fn, win_ref, mix_ref, scale_ref, wout_ref, x)


def _pool_prompt(x, g, win, mix, scale, wout):
    b, t, d = x.shape
    grid = (b, t // ROW_TILE)
    return pl.pallas_call(
        _pool_prompt_body,
        grid=grid,
        in_specs=[
            pl.BlockSpec((1, ROW_TILE, d), lambda i, j: (i, j, 0)),
            _const_spec((1, d)),
            _const_spec(win.shape),
            _const_spec(mix.shape),
            _const_spec((1, POOL_WIDTH)),
            _const_spec(wout.shape),
        ],
        out_specs=[
            pl.BlockSpec((1, ROW_TILE, d), lambda i, j: (i, j, 0)),
            pl.BlockSpec((1, POOL_HALO, POOL_WIDTH), lambda i, j: (i, 0, 0)),
        ],
        out_shape=[
            jax.ShapeDtypeStruct((b, t, d), F32),
            jax.ShapeDtypeStruct((b, POOL_HALO, POOL_WIDTH), F32),
        ],
        scratch_shapes=[pltpu.VMEM((POOL_HALO, POOL_WIDTH), F32)],
        compiler_params=pltpu.CompilerParams(
            dimension_semantics=("parallel", "arbitrary"),
            vmem_limit_bytes=VMEM_LIMIT),
        name="pool_prompt",
    )(x, g, win, mix, scale, wout)


def _pool_sample_body(n_seq, n_new, start, x_ref, st_ref, g_ref, win_ref, mix_ref,
                      scale_ref, wout_ref, x1_ref, u_ref):
    x = x_ref[...]
    hb = _rms_norm(x, g_ref[...]).astype(BF16)

    def pooled_fn(g, u):
        w = POOL_WINDOWS[g]
        cols = slice(g * POOL_GROUP, (g + 1) * POOL_GROUP)
        u_ref[:, cols] = u

        def slab(e):
            if e < POOL_CTX:
                return st_ref[e * n_seq:(e + 1) * n_seq, cols]
            return u[(e - POOL_CTX) * n_seq:(e - POOL_CTX + 1) * n_seq, :]

        outs = []
        for t in range(n_new):
            s = slab(POOL_CTX + t)
            for e in range(POOL_CTX + t - w + 1, POOL_CTX + t):
                s = s + slab(e)
            cnt = min(start + t + 1, w)
            outs.append(s * (1.0 / cnt) - slab(POOL_CTX + t))
        return jnp.concatenate(outs, axis=0)

    x1_ref[...] = _pool_tail(hb, pooled_fn, win_ref, mix_ref, scale_ref, wout_ref, x)


def _pool_sample(x_tm, state_tm, n_seq, n_new, start, g, win, mix, scale, wout):
    rows, d = x_tm.shape
    return pl.pallas_call(
        functools.partial(_pool_sample_body, n_seq, n_new, start),
        out_shape=[
            jax.ShapeDtypeStruct((rows, d), F32),
            jax.ShapeDtypeStruct((rows, POOL_WIDTH), F32),
        ],
        compiler_params=pltpu.CompilerParams(vmem_limit_bytes=VMEM_LIMIT),
        name="pool_sample",
    )(x_tm, state_tm, g, win, mix, scale, wout)


def _project(x, g_ref, w_ref, wf_ref, bf_ref):
    hb = _rms_norm(x, g_ref[...]).astype(BF16)
    parts = [jnp.dot(hb, w_ref[:, i * D_MODEL:(i + 1) * D_MODEL], preferred_element_type=F32)
             for i in range(4)]
    f = jnp.dot(hb, wf_ref[...], preferred_element_type=F32) + bf_ref[...]
    q = parts[0] * (LOG2E * HEAD_DIM ** -0.5)
    return q, parts[1], parts[2], _silu(parts[3]), _log_sigmoid(f)


def _lane_cumsum(x):
    lane = lax.broadcasted_iota(jnp.int32, x.shape, 1)
    sh = 1
    while sh < x.shape[1]:
        x = x + jnp.where(lane >= sh, pltpu.roll(x, sh, 1), 0.0)
        sh *= 2
    return x


def _proj_prompt_body(x_ref, g_ref, w_ref, wf_ref, bf_ref,
                      q_ref, kp_ref, vp_ref, k_ref, v_ref, gate_ref, logf_ref, bias_ref,
                      carry_ref):
    t = pl.program_id(1)

    @pl.when(t == 0)
    def _():
        carry_ref[...] = jnp.zeros_like(carry_ref)

    q, k, v, gate, logf = _project(x_ref[0], g_ref, w_ref, wf_ref, bf_ref)
    k_ref[0] = k
    v_ref[0] = v
    gate_ref[0] = gate.astype(BF16)
    logf_ref[0] = logf[:, :N_HEADS]
    qb, kb, vb = q.astype(BF16), k.astype(BF16), v.astype(BF16)
    for p in range(HEAD_PAIRS):
        cols = slice(p * PAIR_WIDTH, (p + 1) * PAIR_WIDTH)
        q_ref[0, p] = qb[:, cols]
        kp_ref[0, p] = kb[:, cols]
        vp_ref[0, p] = vb[:, cols]

    lt = logf.T[:N_HEADS, :]
    carry = carry_ref[...]
    for c in range(ROW_TILE // 128):
        csum = _lane_cumsum(lt[:, c * 128:(c + 1) * 128]) + carry
        neg = csum * (-LOG2E)
        for p in range(HEAD_PAIRS):
            bias_ref[0, p, 0, :, c * 128:(c + 1) * 128] = neg[2 * p:2 * p + 2, :]
        carry = jnp.broadcast_to(csum[:, 127:128], carry.shape)
    carry_ref[...] = carry


def _proj_prompt(x1, g, w, wf, bf):
    b, t, d = x1.shape
    grid = (b, t // ROW_TILE)
    per_kv = KV_TILE // ROW_TILE
    row_spec = pl.BlockSpec((1, ROW_TILE, d), lambda i, j: (i, j, 0))
    pair_spec = pl.BlockSpec((1, HEAD_PAIRS, ROW_TILE, PAIR_WIDTH), lambda i, j: (i, 0, j, 0))
    pair_shape = jax.ShapeDtypeStruct((b, HEAD_PAIRS, t, PAIR_WIDTH), BF16)
    return pl.pallas_call(
        _proj_prompt_body,
        grid=grid,
        in_specs=[row_spec, _const_spec((1, d)), _const_spec(w.shape),
                  _const_spec(wf.shape), _const_spec(bf.shape)],
        out_specs=[
            pair_spec, pair_spec, pair_spec, row_spec, row_spec, row_spec,
            pl.BlockSpec((1, ROW_TILE, N_HEADS), lambda i, j: (i, j, 0)),
            pl.BlockSpec((1, HEAD_PAIRS, 1, 2, ROW_TILE),
                         lambda i, j: (i, 0, j // per_kv, 0, j % per_kv)),
        ],
        out_shape=[
            pair_shape, pair_shape, pair_shape,
            jax.ShapeDtypeStruct((b, t, d), F32),
            jax.ShapeDtypeStruct((b, t, d), F32),
            jax.ShapeDtypeStruct((b, t, d), BF16),
            jax.ShapeDtypeStruct((b, t, N_HEADS), F32),
            jax.ShapeDtypeStruct((b, HEAD_PAIRS, t // KV_TILE, 2, KV_TILE), F32),
        ],
        scratch_shapes=[pltpu.VMEM((N_HEADS, 128), F32)],
        compiler_params=pltpu.CompilerParams(
            dimension_semantics=("parallel", "arbitrary"),
            vmem_limit_bytes=VMEM_LIMIT),
        name="proj_prompt",
    )(x1, g, w, wf, bf)


def _proj_sample_body(x_ref, g_ref, w_ref, wf_ref, bf_ref,
                      q_ref, k_ref, v_ref, gate_ref, logf_ref):
    q, k, v, gate, logf = _project(x_ref[...], g_ref, w_ref, wf_ref, bf_ref)
    q_ref[...] = q.astype(BF16)
    k_ref[...] = k
    v_ref[...] = v
    gate_ref[...] = gate.astype(BF16)
    logf_ref[...] = logf[:, :N_HEADS]


def _proj_sample(x1, g, w, wf, bf):
    rows, d = x1.shape
    return pl.pallas_call(
        _proj_sample_body,
        out_shape=[
            jax.ShapeDtypeStruct((rows, d), BF16),
            jax.ShapeDtypeStruct((rows, d), F32),
            jax.ShapeDtypeStruct((rows, d), F32),
            jax.ShapeDtypeStruct((rows, d), BF16),
            jax.ShapeDtypeStruct((rows, N_HEADS), F32),
        ],
        compiler_params=pltpu.CompilerParams(vmem_limit_bytes=VMEM_LIMIT),
        name="proj_sample",
    )(x1, g, w, wf, bf)


def _attn_body(n_new, groups, n_sample_steps, pt_ref,
               q_ref, k_ref, v_ref, bias_ref,
               qs_ref, kn_ref, vn_ref, ln_ref, ck_hbm, cv_hbm, cl_hbm,
               o_ref, os_ref,
               kbuf, vbuf, lbuf, sems, qbd_ref, kpad, vpad, lpad,
               m_ref, l_ref, acc_ref, carry_ref):
    qi = pl.program_id(2)
    flat = (pl.program_id(0) * pl.num_programs(1) + pl.program_id(1)) * pl.num_programs(2) + qi
    rows = n_new * N_HEADS

    def copies(sidx, sl):
        bb = sidx // groups
        gg = sidx % groups
        out = []
        for p in range(PAGES_PER_STEP):
            page = pt_ref[bb, (groups - 1 - gg) * PAGES_PER_STEP + p]
            out.append(pltpu.make_async_copy(ck_hbm.at[page], kbuf.at[sl, p], sems.at[0, sl]))
            out.append(pltpu.make_async_copy(cv_hbm.at[page], vbuf.at[sl, p], sems.at[1, sl]))
            out.append(pltpu.make_async_copy(cl_hbm.at[page], lbuf.at[sl, p], sems.at[2, sl]))
        return out

    row_head = lax.broadcasted_iota(jnp.int32, (N_HEADS, D_MODEL), 0)
    lane_head = lax.broadcasted_iota(jnp.int32, (N_HEADS, D_MODEL), 1) // HEAD_DIM
    own = row_head == lane_head

    def suffix_bias(lp_t, carry):
        incl = _lane_cumsum(lp_t)
        total = jnp.broadcast_to(incl[:, 127:128], incl.shape)
        return total - incl + carry, carry + total

    def accumulate(s, pv_fn):
        m_old = m_ref[...]
        m_new = jnp.maximum(m_old, jnp.max(s, axis=1, keepdims=True))
        alpha = jnp.exp2(m_old - m_new)
        p = jnp.exp2(s - m_new)
        l_ref[...] = alpha * l_ref[...] + jnp.sum(p, axis=1, keepdims=True)
        acc_ref[...] = acc_ref[...] * alpha + pv_fn(p.astype(BF16))
        m_ref[...] = m_new

    def sample_begin():
        q = qs_ref[0].astype(F32)
        for t in range(n_new):
            qt = jnp.broadcast_to(q[t:t + 1, :], (N_HEADS, D_MODEL))
            qbd_ref[t * N_HEADS:(t + 1) * N_HEADS, :] = jnp.where(own, qt, 0.0).astype(BF16)
        m_ref[...] = jnp.full_like(m_ref, NEG_BIG)
        l_ref[...] = jnp.zeros_like(l_ref)
        acc_ref[...] = jnp.zeros_like(acc_ref)
        kpad[0:n_new, :] = kn_ref[0]
        vpad[0:n_new, :] = vn_ref[0]
        lpad[0:n_new, :] = ln_ref[0]
        eye = (lax.broadcasted_iota(jnp.int32, (N_HEADS, N_HEADS), 0) ==
               lax.broadcasted_iota(jnp.int32, (N_HEADS, N_HEADS), 1)).astype(F32)
        lp_t = lax.dot_general(eye, lpad[...], (((1,), (1,)), ((), ())),
                               precision=lax.Precision.HIGHEST, preferred_element_type=F32)
        bias, carry = suffix_bias(lp_t * LOG2E, jnp.zeros((N_HEADS, 128), F32))
        carry_ref[...] = carry
        s = lax.dot_general(qbd_ref[...], kpad[...].astype(BF16), (((1,), (1,)), ((), ())),
                            preferred_element_type=F32)
        s = s + jnp.concatenate([bias] * n_new, axis=0)
        q_idx = lax.broadcasted_iota(jnp.int32, (rows, PAGE), 0) // N_HEADS
        k_idx = lax.broadcasted_iota(jnp.int32, (rows, PAGE), 1)
        s = jnp.where(k_idx <= q_idx, s, NEG_BIG)
        vb = vpad[...].astype(BF16)
        accumulate(s, lambda p: jnp.dot(p, vb, preferred_element_type=F32))

    def sample_pages(slot):
        carry = carry_ref[...]
        biases = [None] * PAGES_PER_STEP
        for p in reversed(range(PAGES_PER_STEP)):
            biases[p], carry = suffix_bias(lbuf[slot, p] * LOG2E, carry)
        carry_ref[...] = carry
        bias = jnp.concatenate(biases, axis=1)
        kcat = jnp.concatenate([kbuf[slot, p].astype(BF16) for p in range(PAGES_PER_STEP)], axis=1)
        s = jnp.dot(qbd_ref[...], kcat, preferred_element_type=F32)
        s = s + jnp.concatenate([bias] * n_new, axis=0)
        vcat = jnp.concatenate([vbuf[slot, p].astype(BF16) for p in range(PAGES_PER_STEP)], axis=1)
        accumulate(s, lambda p: lax.dot_general(p, vcat, (((1,), (1,)), ((), ())),
                                                preferred_element_type=F32))

    def sample_end():
        acc = acc_ref[...] * (1.0 / l_ref[...])
        for t in range(n_new):
            blk = jnp.where(own, acc[t * N_HEADS:(t + 1) * N_HEADS, :], 0.0)
            os_ref[0, t:t + 1, :] = jnp.sum(blk, axis=0, keepdims=True)

    @pl.when(flat == 0)
    def _():
        for u in range(SAMPLE_STEPS):
            for c in copies(u, u):
                c.start()
        kpad[...] = jnp.zeros_like(kpad)
        vpad[...] = jnp.zeros_like(vpad)
        lpad[...] = jnp.zeros_like(lpad)

    first_sidx = flat * SAMPLE_STEPS

    def refill_of(u):
        return (first_sidx + u + SAMPLE_STEPS) % n_sample_steps

    def sample_step(u):
        for c in copies(first_sidx + u, u):
            c.wait()
        sample_pages(u)
        for c in copies(refill_of(u), u):
            c.start()

    q = q_ref[0, 0]
    lane = lax.broadcasted_iota(jnp.int32, (1, PAIR_WIDTH), 1)
    first = lane < HEAD_DIM
    zero = jnp.zeros_like(q)
    q_heads = (jnp.where(first, q, zero), jnp.where(first, zero, q))
    half = Q_TILE // 2
    causal = (lax.broadcasted_iota(jnp.int32, (half, KV_TILE), 0) >=
              lax.broadcasted_iota(jnp.int32, (half, KV_TILE), 1))

    def head_update(qh, kt, vt, brow, m, l, mask_rows):
        s = lax.dot_general(qh, kt, (((1,), (1,)), ((), ())), preferred_element_type=F32)
        s = s + brow
        if mask_rows is not None:
            top = jnp.where(causal, s[:mask_rows], NEG_BIG)
            s = top if mask_rows == s.shape[0] else jnp.concatenate([top, s[mask_rows:]], axis=0)
        m_new = jnp.maximum(m, jnp.max(s, axis=1, keepdims=True))
        alpha = jnp.exp2(m - m_new)
        p = jnp.exp2(s - m_new)
        l_new = alpha * l + jnp.sum(p, axis=1, keepdims=True)
        pv = jnp.dot(p.astype(BF16), vt, preferred_element_type=F32)
        return m_new, l_new, alpha, pv

    def key_tile(t, carry, rows_, mask_rows):
        m0, l0, m1, l1, acc = carry
        start = pl.multiple_of(t * KV_TILE, KV_TILE)
        kt = k_ref[0, 0, pl.ds(start, KV_TILE), :]
        vt = v_ref[0, 0, pl.ds(start, KV_TILE), :]
        bt = bias_ref[0, 0, t]
        m0, l0, a0, pv0 = head_update(q_heads[0][rows_], kt, vt, bt[0:1, :], m0, l0, mask_rows)
        m1, l1, a1, pv1 = head_update(q_heads[1][rows_], kt, vt, bt[1:2, :], m1, l1, mask_rows)
        acc = acc * jnp.where(first, a0, a1) + jnp.where(first, pv0, pv1)
        return m0, l0, m1, l1, acc

    everything = slice(0, Q_TILE)

    def visible_pair(i, carry):
        carry = key_tile(2 * i, carry, everything, None)
        return key_tile(2 * i + 1, carry, everything, None)

    neg = jnp.full((Q_TILE, 1), NEG_BIG, F32)
    zcol = jnp.zeros((Q_TILE, 1), F32)
    init = (neg, zcol, neg, zcol, jnp.zeros((Q_TILE, PAIR_WIDTH), F32))
    carry = lax.fori_loop(0, qi, visible_pair, init)
    pl.when(first_sidx % groups == 0)(sample_begin)
    for u in range(SAMPLE_STEPS):
        sample_step(u)
    m0, l0, m1, l1, acc = key_tile(2 * qi, carry, everything, half)
    lower = slice(half, Q_TILE)
    low = key_tile(2 * qi + 1, (m0[lower], l0[lower], m1[lower], l1[lower], acc[lower]),
                   lower, half)
    o_ref[0, :half, :] = (acc[:half] * jnp.where(first, 1.0 / l0[:half], 1.0 / l1[:half])
                          ).astype(BF16)
    o_ref[0, half:, :] = (low[4] * jnp.where(first, 1.0 / low[1], 1.0 / low[3])).astype(BF16)
    pl.when((first_sidx + SAMPLE_STEPS) % groups == 0)(sample_end)

    @pl.when(first_sidx + SAMPLE_STEPS == n_sample_steps)
    def _():
        for u in range(SAMPLE_STEPS):
            for c in copies(refill_of(u), u):
                c.wait()


def _attention(q, k, v, bias, page_table, qs, k_new, v_new, logf_new,
               cache_kt, cache_vt, cache_lt):
    b, hp, t, w = q.shape
    n_seq, n_new, d = qs.shape
    groups = page_table.shape[1] // PAGES_PER_STEP
    rows = n_new * N_HEADS
    grid = (b, hp, t // Q_TILE)
    n_grid = b * hp * (t // Q_TILE)
    n_sample_steps = n_seq * groups
    assert Q_TILE == 2 * KV_TILE and KV_TILE == ROW_TILE
    assert n_sample_steps == n_grid * SAMPLE_STEPS and groups % SAMPLE_STEPS == 0

    def seq_of(i, p, j):
        return (((i * hp + p) * (t // Q_TILE) + j) * SAMPLE_STEPS) // groups

    new_spec = pl.BlockSpec((1, n_new, d), lambda i, p, j, pt: (seq_of(i, p, j), 0, 0))
    any_spec = pl.BlockSpec(memory_space=pl.ANY)
    grid_spec = pltpu.PrefetchScalarGridSpec(
        num_scalar_prefetch=1,
        grid=grid,
        in_specs=[
            pl.BlockSpec((1, 1, Q_TILE, w), lambda i, p, j, pt: (i, p, j, 0)),
            pl.BlockSpec((1, 1, t, w), lambda i, p, j, pt: (i, p, 0, 0)),
            pl.BlockSpec((1, 1, t, w), lambda i, p, j, pt: (i, p, 0, 0)),
            pl.BlockSpec((1, 1, t // KV_TILE, 2, KV_TILE), lambda i, p, j, pt: (i, p, 0, 0, 0)),
            new_spec, new_spec, new_spec,
            pl.BlockSpec((1, n_new, N_HEADS), lambda i, p, j, pt: (seq_of(i, p, j), 0, 0)),
            any_spec, any_spec, any_spec,
        ],
        out_specs=[
            pl.BlockSpec((1, Q_TILE, w), lambda i, p, j, pt: (i, j, p)),
            new_spec,
        ],
        scratch_shapes=[
            pltpu.VMEM((2, PAGES_PER_STEP, d, PAGE), F32),
            pltpu.VMEM((2, PAGES_PER_STEP, d, PAGE), F32),
            pltpu.VMEM((2, PAGES_PER_STEP, N_HEADS, PAGE), F32),
            pltpu.SemaphoreType.DMA((3, 2)),
            pltpu.VMEM((rows, d), BF16),
            pltpu.VMEM((PAGE, d), F32),
            pltpu.VMEM((PAGE, d), F32),
            pltpu.VMEM((PAGE, N_HEADS), F32),
            pltpu.VMEM((rows, 1), F32),
            pltpu.VMEM((rows, 1), F32),
            pltpu.VMEM((rows, d), F32),
            pltpu.VMEM((N_HEADS, 128), F32),
        ],
    )
    return pl.pallas_call(
        functools.partial(_attn_body, n_new, groups, n_sample_steps),
        grid_spec=grid_spec,
        out_shape=[jax.ShapeDtypeStruct((b, t, hp * w), BF16),
                   jax.ShapeDtypeStruct((n_seq, n_new, d), F32)],
        compiler_params=pltpu.CompilerParams(
            dimension_semantics=("arbitrary", "arbitrary", "arbitrary"),
            vmem_limit_bytes=VMEM_LIMIT),
        name="attention",
    )(page_table, q, k, v, bias, qs, k_new, v_new, logf_new, cache_kt, cache_vt, cache_lt)


def _out_body(o_ref, gate_ref, x1_ref, w_ref, g_ref, y_ref):
    og = (o_ref[...].astype(F32) * gate_ref[...].astype(F32)).astype(BF16)
    x2 = x1_ref[...] + jnp.dot(og, w_ref[...], preferred_element_type=F32)
    y_ref[...] = _rms_norm(x2, g_ref[...])


def _out_proj(o, gate, x1, w, g):
    rows, d = x1.shape
    tile = min(OUT_TILE, rows)
    row_spec = pl.BlockSpec((tile, d), lambda i: (i, 0))
    return pl.pallas_call(
        _out_body,
        grid=(rows // tile,),
        in_specs=[row_spec, row_spec, row_spec, _const_spec(w.shape), _const_spec((1, d))],
        out_specs=row_spec,
        out_shape=jax.ShapeDtypeStruct((rows, d), F32),
        compiler_params=pltpu.CompilerParams(
            dimension_semantics=("parallel",), vmem_limit_bytes=VMEM_LIMIT),
        name="out_proj",
    )(o, gate, x1, w, g)


def kernel(x_prompt, x_sample, state_pool, cache_k, cache_v, cache_logf, page_table,
           norm_g, w_in_pool, pool_mix, pool_scale, w_out_pool,
           w_in_attn, b_forget, w_out_attn, final_norm):
    b, t, d = x_prompt.shape
    n_seq, n_new, _ = x_sample.shape
    n_phys = cache_k.shape[1]
    past = page_table.shape[1] * cache_k.shape[2]
    assert norm_g.shape[0] == 2 and d == D_MODEL and cache_k.shape[2] == PAGE

    g0 = norm_g[0].reshape(1, d)
    g1 = norm_g[1].reshape(1, d)
    gf = final_norm.reshape(1, d)
    win = w_in_pool[0].astype(BF16)
    mix = pool_mix[0].astype(BF16)
    scale = pool_scale[0].reshape(1, POOL_WIDTH)
    wout_pool = w_out_pool[0].astype(BF16)
    w_attn = w_in_attn[0, :, :4 * d].astype(BF16)
    wf = jnp.pad(w_in_attn[0, :, 4 * d:], ((0, 0), (0, 128 - N_HEADS))).astype(BF16)
    bf = jnp.pad(b_forget[0], (0, 128 - N_HEADS)).reshape(1, 128)
    wout_attn = w_out_attn[0].astype(BF16)

    x1p, last = _pool_prompt(x_prompt, g0, win, mix, scale, wout_pool)
    xs_tm = x_sample.transpose(1, 0, 2).reshape(n_new * n_seq, d)
    st_tm = state_pool[0].transpose(1, 0, 2).reshape(POOL_CTX * n_seq, POOL_WIDTH)
    x1s_tm, u_tm = _pool_sample(xs_tm, st_tm, n_seq, n_new, past, g0, win, mix, scale, wout_pool)
    x1s = x1s_tm.reshape(n_new, n_seq, d).transpose(1, 0, 2).reshape(n_seq * n_new, d)
    u_new = u_tm.reshape(n_new, n_seq, POOL_WIDTH).transpose(1, 0, 2)

    qp, kp, vp, k32, v32, gate_p, logf_p, bias_p = _proj_prompt(x1p, g1, w_attn, wf, bf)
    qs, ks, vs, gate_s, logf_s = _proj_sample(x1s, g1, w_attn, wf, bf)
    o_p, o_s = _attention(qp, kp, vp, bias_p, page_table,
                          qs.reshape(n_seq, n_new, d), ks.reshape(n_seq, n_new, d),
                          vs.reshape(n_seq, n_new, d), logf_s.reshape(n_seq, n_new, N_HEADS),
                          cache_k[0].transpose(0, 2, 3, 1).reshape(n_phys, d, PAGE),
                          cache_v[0].transpose(0, 2, 3, 1).reshape(n_phys, d, PAGE),
                          cache_logf[0].transpose(0, 2, 1))
    y_p = _out_proj(o_p.reshape(b * t, d), gate_p.reshape(b * t, d), x1p.reshape(b * t, d),
                    wout_attn, gf).reshape(b, t, d)
    y_s = _out_proj(o_s.reshape(n_seq * n_new, d).astype(BF16), gate_s, x1s, wout_attn, gf)

    pool_prompt = last[:, POOL_HALO - POOL_CTX:, :][None]
    pool_sample = jnp.concatenate([state_pool[0][:, n_new:, :], u_new], axis=1)[None]
    heads = (N_HEADS, HEAD_DIM)
    return (y_p, y_s.reshape(n_seq, n_new, d), pool_prompt, pool_sample,
            k32.reshape(1, b, t, *heads), v32.reshape(1, b, t, *heads), logf_p[None],
            ks.reshape(1, n_seq, n_new, *heads), vs.reshape(1, n_seq, n_new, *heads),
            logf_s.reshape(1, n_seq, n_new, N_HEADS))
```

```python
import functools

import jax
import jax.numpy as jnp
from jax import lax
from jax.experimental import pallas as pl
from jax.experimental.pallas import tpu as pltpu

F32 = jnp.float32
BF16 = jnp.bfloat16

D_MODEL = 1024
POOL_WIDTH = 2048
POOL_WINDOWS = (2, 4, 8, 16)
POOL_GROUP = POOL_WIDTH // len(POOL_WINDOWS)
POOL_CTX = max(POOL_WINDOWS) - 1
POOL_HALO = 16
N_HEADS = 16
HEAD_DIM = 64
HEAD_PAIRS = N_HEADS // 2
PAIR_WIDTH = 2 * HEAD_DIM
RMS_EPS = 1e-6
NEG_BIG = -1e30
LOG2E = 1.4426950408889634

ROW_TILE = 512
OUT_TILE = 1024
Q_TILE = 1024
KV_TILE = 512
PAGE = 128
PAGES_PER_STEP = 8
SAMPLE_STEPS = 4
PAGE_SLOTS = 2

VMEM_LIMIT = 56 * 1024 * 1024


def _rms_norm(x, g):
    ms = jnp.mean(x * x, axis=-1, keepdims=True)
    return x * lax.rsqrt(ms + RMS_EPS) * g


def _silu(z):
    return z * jax.nn.sigmoid(z)


def _log_sigmoid(x):
    return jnp.minimum(x, 0.0) - jnp.log1p(jnp.exp(-jnp.abs(x)))


def _const_spec(shape):
    return pl.BlockSpec(shape, lambda *_: (0,) * len(shape), pipeline_mode=pl.Buffered(1))


def _pool_tail(hb, pooled_fn, win_ref, mix_ref, scale_ref, wout_ref, resid):
    acc = resid
    for g in range(len(POOL_WINDOWS)):
        cols = slice(g * POOL_GROUP, (g + 1) * POOL_GROUP)
        zcols = slice(POOL_WIDTH + g * POOL_GROUP, POOL_WIDTH + (g + 1) * POOL_GROUP)
        u = jnp.dot(hb, win_ref[:, cols], preferred_element_type=F32)
        pooled = pooled_fn(g, u)
        y = jnp.dot(pooled.astype(BF16), mix_ref[g], preferred_element_type=F32)
        y = y * scale_ref[:, cols]
        z = jnp.dot(hb, win_ref[:, zcols], preferred_element_type=F32)
        yg = (y * _silu(z)).astype(BF16)
        acc = acc + jnp.dot(yg, wout_ref[cols, :], preferred_element_type=F32)
    return acc


def _pool_prompt_body(x_ref, g_ref, win_ref, mix_ref, scale_ref, wout_ref,
                      x1_ref, last_ref, halo_ref):
    t = pl.program_id(1)

    @pl.when(t == 0)
    def _():
        halo_ref[...] = jnp.zeros_like(halo_ref)

    x = x_ref[0]
    hb = _rms_norm(x, g_ref[...]).astype(BF16)
    pos = t * ROW_TILE + lax.broadcasted_iota(jnp.int32, (ROW_TILE, 1), 0)

    def pooled_fn(g, u):
        w = POOL_WINDOWS[g]
        cols = slice(g * POOL_GROUP, (g + 1) * POOL_GROUP)
        e = jnp.concatenate([halo_ref[:, cols], u], axis=0)
        tail = u[ROW_TILE - POOL_HALO:, :]
        halo_ref[:, cols] = tail
        last_ref[0, :, cols] = tail
        sh = 1
        while sh < w:
            e = e + pltpu.roll(e, sh, 0)
            sh *= 2
        inv = 1.0 / jnp.minimum(pos + 1, w).astype(F32)
        return e[POOL_HALO:, :] * inv - u

    x1_ref[0] = _pool_tail(hb, pooled_fn, win_ref, mix_ref, scale_ref, wout_ref, x)


def _pool_prompt(x, g, win, mix, scale, wout):
    b, t, d = x.shape
    grid = (b, t // ROW_TILE)
    return pl.pallas_call(
        _pool_prompt_body,
        grid=grid,
        in_specs=[
            pl.BlockSpec((1, ROW_TILE, d), lambda i, j: (i, j, 0)),
            _const_spec((1, d)),
            _const_spec(win.shape),
            _const_spec(mix.shape),
            _const_spec((1, POOL_WIDTH)),
            _const_spec(wout.shape),
        ],
        out_specs=[
            pl.BlockSpec((1, ROW_TILE, d), lambda i, j: (i, j, 0)),
            pl.BlockSpec((1, POOL_HALO, POOL_WIDTH), lambda i, j: (i, 0, 0)),
        ],
        out_shape=[
            jax.ShapeDtypeStruct((b, t, d), F32),
            jax.ShapeDtypeStruct((b, POOL_HALO, POOL_WIDTH), F32),
        ],
        scratch_shapes=[pltpu.VMEM((POOL_HALO, POOL_WIDTH), F32)],
        compiler_params=pltpu.CompilerParams(
            dimension_semantics=("parallel", "arbitrary"),
            vmem_limit_bytes=VMEM_LIMIT),
        name="pool_prompt",
    )(x, g, win, mix, scale, wout)


def _pool_sample_body(n_seq, n_new, start, x_ref, st_ref, g_ref, win_ref, mix_ref,
                      scale_ref, wout_ref, x1_ref, u_ref):
    x = x_ref[...]
    hb = _rms_norm(x, g_ref[...]).astype(BF16)

    def pooled_fn(g, u):
        w = POOL_WINDOWS[g]
        cols = slice(g * POOL_GROUP, (g + 1) * POOL_GROUP)
        u_ref[:, cols] = u

        def slab(e):
            if e < POOL_CTX:
                return st_ref[e * n_seq:(e + 1) * n_seq, cols]
            return u[(e - POOL_CTX) * n_seq:(e - POOL_CTX + 1) * n_seq, :]

        outs = []
        for t in range(n_new):
            s = slab(POOL_CTX + t)
            for e in range(POOL_CTX + t - w + 1, POOL_CTX + t):
                s = s + slab(e)
            cnt = min(start + t + 1, w)
            outs.append(s * (1.0 / cnt) - slab(POOL_CTX + t))
        return jnp.concatenate(outs, axis=0)

    x1_ref[...] = _pool_tail(hb, pooled_fn, win_ref, mix_ref, scale_ref, wout_ref, x)


def _pool_sample(x_tm, state_tm, n_seq, n_new, start, g, win, mix, scale, wout):
    rows, d = x_tm.shape
    return pl.pallas_call(
        functools.partial(_pool_sample_body, n_seq, n_new, start),
        out_shape=[
            jax.ShapeDtypeStruct((rows, d), F32),
            jax.ShapeDtypeStruct((rows, POOL_WIDTH), F32),
        ],
        compiler_params=pltpu.CompilerParams(vmem_limit_bytes=VMEM_LIMIT),
        name="pool_sample",
    )(x_tm, state_tm, g, win, mix, scale, wout)


def _project(x, g_ref, w_ref, wf_ref, bf_ref):
    hb = _rms_norm(x, g_ref[...]).astype(BF16)
    parts = [jnp.dot(hb, w_ref[:, i * D_MODEL:(i + 1) * D_MODEL], preferred_element_type=F32)
             for i in range(4)]
    f = jnp.dot(hb, wf_ref[...], preferred_element_type=F32) + bf_ref[...]
    q = parts[0] * (LOG2E * HEAD_DIM ** -0.5)
    return q, parts[1], parts[2], _silu(parts[3]), _log_sigmoid(f)


def _lane_cumsum(x):
    lane = lax.broadcasted_iota(jnp.int32, x.shape, 1)
    sh = 1
    while sh < x.shape[1]:
        x = x + jnp.where(lane >= sh, pltpu.roll(x, sh, 1), 0.0)
        sh *= 2
    return x


def _proj_prompt_body(x_ref, g_ref, w_ref, wf_ref, bf_ref,
                      q_ref, kp_ref, vp_ref, k_ref, v_ref, gate_ref, logf_ref, bias_ref,
                      carry_ref):
    t = pl.program_id(1)

    @pl.when(t == 0)
    def _():
        carry_ref[...] = jnp.zeros_like(carry_ref)

    q, k, v, gate, logf = _project(x_ref[0], g_ref, w_ref, wf_ref, bf_ref)
    k_ref[0] = k
    v_ref[0] = v
    gate_ref[0] = gate.astype(BF16)
    logf_ref[0] = logf[:, :N_HEADS]
    qb, kb, vb = q.astype(BF16), k.astype(BF16), v.astype(BF16)
    for p in range(HEAD_PAIRS):
        cols = slice(p * PAIR_WIDTH, (p + 1) * PAIR_WIDTH)
        q_ref[0, p] = qb[:, cols]
        kp_ref[0, p] = kb[:, cols]
        vp_ref[0, p] = vb[:, cols]

    lt = logf.T[:N_HEADS, :]
    carry = carry_ref[...]
    for c in range(ROW_TILE // 128):
        csum = _lane_cumsum(lt[:, c * 128:(c + 1) * 128]) + carry
        neg = csum * (-LOG2E)
        for p in range(HEAD_PAIRS):
            bias_ref[0, p, 0, :, c * 128:(c + 1) * 128] = neg[2 * p:2 * p + 2, :]
        carry = jnp.broadcast_to(csum[:, 127:128], carry.shape)
    carry_ref[...] = carry


def _proj_prompt(x1, g, w, wf, bf):
    b, t, d = x1.shape
    grid = (b, t // ROW_TILE)
    per_kv = KV_TILE // ROW_TILE
    row_spec = pl.BlockSpec((1, ROW_TILE, d), lambda i, j: (i, j, 0))
    pair_spec = pl.BlockSpec((1, HEAD_PAIRS, ROW_TILE, PAIR_WIDTH), lambda i, j: (i, 0, j, 0))
    pair_shape = jax.ShapeDtypeStruct((b, HEAD_PAIRS, t, PAIR_WIDTH), BF16)
    return pl.pallas_call(
        _proj_prompt_body,
        grid=grid,
        in_specs=[row_spec, _const_spec((1, d)), _const_spec(w.shape),
                  _const_spec(wf.shape), _const_spec(bf.shape)],
        out_specs=[
            pair_spec, pair_spec, pair_spec, row_spec, row_spec, row_spec,
            pl.BlockSpec((1, ROW_TILE, N_HEADS), lambda i, j: (i, j, 0)),
            pl.BlockSpec((1, HEAD_PAIRS, 1, 2, ROW_TILE),
                         lambda i, j: (i, 0, j // per_kv, 0, j % per_kv)),
        ],
        out_shape=[
            pair_shape, pair_shape, pair_shape,
            jax.ShapeDtypeStruct((b, t, d), F32),
            jax.ShapeDtypeStruct((b, t, d), F32),
            jax.ShapeDtypeStruct((b, t, d), BF16),
            jax.ShapeDtypeStruct((b, t, N_HEADS), F32),
            jax.ShapeDtypeStruct((b, HEAD_PAIRS, t // KV_TILE, 2, KV_TILE), F32),
        ],
        scratch_shapes=[pltpu.VMEM((N_HEADS, 128), F32)],
        compiler_params=pltpu.CompilerParams(
            dimension_semantics=("parallel", "arbitrary"),
            vmem_limit_bytes=VMEM_LIMIT),
        name="proj_prompt",
    )(x1, g, w, wf, bf)


def _proj_sample_body(x_ref, g_ref, w_ref, wf_ref, bf_ref,
                      q_ref, k_ref, v_ref, gate_ref, logf_ref):
    q, k, v, gate, logf = _project(x_ref[...], g_ref, w_ref, wf_ref, bf_ref)
    q_ref[...] = q.astype(BF16)
    k_ref[...] = k
    v_ref[...] = v
    gate_ref[...] = gate.astype(BF16)
    logf_ref[...] = logf[:, :N_HEADS]


def _proj_sample(x1, g, w, wf, bf):
    rows, d = x1.shape
    return pl.pallas_call(
        _proj_sample_body,
        out_shape=[
            jax.ShapeDtypeStruct((rows, d), BF16),
            jax.ShapeDtypeStruct((rows, d), F32),
            jax.ShapeDtypeStruct((rows, d), F32),
            jax.ShapeDtypeStruct((rows, d), BF16),
            jax.ShapeDtypeStruct((rows, N_HEADS), F32),
        ],
        compiler_params=pltpu.CompilerParams(vmem_limit_bytes=VMEM_LIMIT),
        name="proj_sample",
    )(x1, g, w, wf, bf)


def _attn_body(n_new, groups, n_sample_steps, pt_ref,
               q_ref, k_ref, v_ref, bias_ref,
               qs_ref, kn_ref, vn_ref, ln_ref, ck_hbm, cv_hbm, cl_hbm,
               o_ref, os_ref,
               kbuf, vbuf, lbuf, sems, qbd_ref, kpad, vpad, lpad,
               m_ref, l_ref, acc_ref, carry_ref):
    flat = pl.program_id(0) * pl.num_programs(1) + pl.program_id(1)
    rows = n_new * N_HEADS

    def copies(sidx, sl):
        bb = sidx // groups
        gg = sidx % groups
        out = []
        for p in range(PAGES_PER_STEP):
            page = pt_ref[bb, (groups - 1 - gg) * PAGES_PER_STEP + p]
            out.append(pltpu.make_async_copy(ck_hbm.at[page], kbuf.at[sl, p], sems.at[0, sl]))
            out.append(pltpu.make_async_copy(cv_hbm.at[page], vbuf.at[sl, p], sems.at[1, sl]))
            out.append(pltpu.make_async_copy(cl_hbm.at[page], lbuf.at[sl, p], sems.at[2, sl]))
        return out

    row_head = lax.broadcasted_iota(jnp.int32, (N_HEADS, D_MODEL), 0)
    lane_head = lax.broadcasted_iota(jnp.int32, (N_HEADS, D_MODEL), 1) // HEAD_DIM
    own = row_head == lane_head

    def suffix_bias(lp_t, carry):
        incl = _lane_cumsum(lp_t)
        total = jnp.broadcast_to(incl[:, 127:128], incl.shape)
        return total - incl + carry, carry + total

    def accumulate(s, pv_fn):
        m_old = m_ref[...]
        m_new = jnp.maximum(m_old, jnp.max(s, axis=1, keepdims=True))
        alpha = jnp.exp2(m_old - m_new)
        p = jnp.exp2(s - m_new)
        l_ref[...] = alpha * l_ref[...] + jnp.sum(p, axis=1, keepdims=True)
        acc_ref[...] = acc_ref[...] * alpha + pv_fn(p.astype(BF16))
        m_ref[...] = m_new

    def sample_begin():
        q = qs_ref[0].astype(F32)
        for t in range(n_new):
            qt = jnp.broadcast_to(q[t:t + 1, :], (N_HEADS, D_MODEL))
            qbd_ref[t * N_HEADS:(t + 1) * N_HEADS, :] = jnp.where(own, qt, 0.0).astype(BF16)
        m_ref[...] = jnp.full_like(m_ref, NEG_BIG)
        l_ref[...] = jnp.zeros_like(l_ref)
        acc_ref[...] = jnp.zeros_like(acc_ref)
        kpad[0:n_new, :] = kn_ref[0]
        vpad[0:n_new, :] = vn_ref[0]
        lpad[0:n_new, :] = ln_ref[0]
        eye = (lax.broadcasted_iota(jnp.int32, (N_HEADS, N_HEADS), 0) ==
               lax.broadcasted_iota(jnp.int32, (N_HEADS, N_HEADS), 1)).astype(F32)
        lp_t = lax.dot_general(eye, lpad[...], (((1,), (1,)), ((), ())),
                               precision=lax.Precision.HIGHEST, preferred_element_type=F32)
        bias, carry = suffix_bias(lp_t * LOG2E, jnp.zeros((N_HEADS, 128), F32))
        carry_ref[...] = carry
        s = lax.dot_general(qbd_ref[...], kpad[...].astype(BF16), (((1,), (1,)), ((), ())),
                            preferred_element_type=F32)
        s = s + jnp.concatenate([bias] * n_new, axis=0)
        q_idx = lax.broadcasted_iota(jnp.int32, (rows, PAGE), 0) // N_HEADS
        k_idx = lax.broadcasted_iota(jnp.int32, (rows, PAGE), 1)
        s = jnp.where(k_idx <= q_idx, s, NEG_BIG)
        vb = vpad[...].astype(BF16)
        accumulate(s, lambda p: jnp.dot(p, vb, preferred_element_type=F32))

    def sample_pages(slot):
        carry = carry_ref[...]
        biases = [None] * PAGES_PER_STEP
        for p in reversed(range(PAGES_PER_STEP)):
            biases[p], carry = suffix_bias(lbuf[slot, p] * LOG2E, carry)
        carry_ref[...] = carry
        bias = jnp.concatenate(biases, axis=1)
        kcat = jnp.concatenate([kbuf[slot, p].astype(BF16) for p in range(PAGES_PER_STEP)], axis=1)
        s = jnp.dot(qbd_ref[...], kcat, preferred_element_type=F32)
        s = s + jnp.concatenate([bias] * n_new, axis=0)
        vcat = jnp.concatenate([vbuf[slot, p].astype(BF16) for p in range(PAGES_PER_STEP)], axis=1)
        accumulate(s, lambda p: lax.dot_general(p, vcat, (((1,), (1,)), ((), ())),
                                                preferred_element_type=F32))

    def sample_end():
        acc = acc_ref[...] * (1.0 / l_ref[...])
        for t in range(n_new):
            blk = jnp.where(own, acc[t * N_HEADS:(t + 1) * N_HEADS, :], 0.0)
            os_ref[0, t:t + 1, :] = jnp.sum(blk, axis=0, keepdims=True)

    @pl.when(flat == 0)
    def _():
        for u in range(PAGE_SLOTS):
            for c in copies(u, u):
                c.start()
        kpad[...] = jnp.zeros_like(kpad)
        vpad[...] = jnp.zeros_like(vpad)
        lpad[...] = jnp.zeros_like(lpad)

    first_sidx = flat * SAMPLE_STEPS

    def refill_of(u):
        return (first_sidx + u + PAGE_SLOTS) % n_sample_steps

    def sample_step(u):
        slot = u % PAGE_SLOTS
        for c in copies(first_sidx + u, slot):
            c.wait()
        sample_pages(slot)
        for c in copies(refill_of(u), slot):
            c.start()

    lane = lax.broadcasted_iota(jnp.int32, (1, PAIR_WIDTH), 1)
    first = lane < HEAD_DIM
    half = Q_TILE // 2
    causal = (lax.broadcasted_iota(jnp.int32, (half, KV_TILE), 0) >=
              lax.broadcasted_iota(jnp.int32, (half, KV_TILE), 1))

    def query_heads(qi):
        q = q_ref[0, 0, qi * Q_TILE:(qi + 1) * Q_TILE, :]
        zero = jnp.zeros_like(q)
        return jnp.where(first, q, zero), jnp.where(first, zero, q)

    def head_update(qh, kt, vt, brow, m, l, mask_rows):
        s = lax.dot_general(qh, kt, (((1,), (1,)), ((), ())), preferred_element_type=F32)
        s = s + brow
        if mask_rows is not None:
            top = jnp.where(causal, s[:mask_rows], NEG_BIG)
            s = top if mask_rows == s.shape[0] else jnp.concatenate([top, s[mask_rows:]], axis=0)
        m_new = jnp.maximum(m, jnp.max(s, axis=1, keepdims=True))
        alpha = jnp.exp2(m - m_new)
        p = jnp.exp2(s - m_new)
        l_new = alpha * l + jnp.sum(p, axis=1, keepdims=True)
        pv = jnp.dot(p.astype(BF16), vt, preferred_element_type=F32)
        return m_new, l_new, alpha, pv

    def key_tile(q_heads, t, carry, rows_, mask_rows):
        m0, l0, m1, l1, acc = carry
        kt = k_ref[0, 0, t * KV_TILE:(t + 1) * KV_TILE, :]
        vt = v_ref[0, 0, t * KV_TILE:(t + 1) * KV_TILE, :]
        bt = bias_ref[0, 0, t]
        m0, l0, a0, pv0 = head_update(q_heads[0][rows_], kt, vt, bt[0:1, :], m0, l0, mask_rows)
        m1, l1, a1, pv1 = head_update(q_heads[1][rows_], kt, vt, bt[1:2, :], m1, l1, mask_rows)
        acc = acc * jnp.where(first, a0, a1) + jnp.where(first, pv0, pv1)
        return m0, l0, m1, l1, acc

    everything = slice(0, Q_TILE)
    neg = jnp.full((Q_TILE, 1), NEG_BIG, F32)
    zcol = jnp.zeros((Q_TILE, 1), F32)
    init = (neg, zcol, neg, zcol, jnp.zeros((Q_TILE, PAIR_WIDTH), F32))

    def visible_pair(q_heads, i, carry):
        carry = key_tile(q_heads, 2 * i, carry, everything, None)
        return key_tile(q_heads, 2 * i + 1, carry, everything, None)

    def diagonal_block(q_heads, qi, carry):
        m0, l0, m1, l1, acc = key_tile(q_heads, 2 * qi, carry, everything, half)
        lower = slice(half, Q_TILE)
        low = key_tile(q_heads, 2 * qi + 1,
                       (m0[lower], l0[lower], m1[lower], l1[lower], acc[lower]), lower, half)
        base = qi * Q_TILE
        o_ref[0, base:base + half, :] = (
            acc[:half] * jnp.where(first, 1.0 / l0[:half], 1.0 / l1[:half])).astype(BF16)
        o_ref[0, base + half:base + Q_TILE, :] = (
            low[4] * jnp.where(first, 1.0 / low[1], 1.0 / low[3])).astype(BF16)

    n_q = q_ref.shape[2] // Q_TILE
    phases = []
    for qi in range(n_q):
        for i in range(qi):
            phases.append(("pair", qi, i))
        phases.append(("diag", qi, None))
    assert len(phases) + 1 == SAMPLE_STEPS

    pl.when(first_sidx % groups == 0)(sample_begin)
    sample_step(0)
    q_heads, carry = None, None
    for n, (kind, qi, i) in enumerate(phases):
        if kind == "diag" and qi == 0 or kind == "pair" and i == 0:
            q_heads, carry = query_heads(qi), init
        if kind == "pair":
            carry = visible_pair(q_heads, i, carry)
        else:
            diagonal_block(q_heads, qi, carry)
        sample_step(n + 1)
    pl.when((first_sidx + SAMPLE_STEPS) % groups == 0)(sample_end)

    @pl.when(first_sidx + SAMPLE_STEPS == n_sample_steps)
    def _():
        for u in range(SAMPLE_STEPS - PAGE_SLOTS, SAMPLE_STEPS):
            for c in copies(refill_of(u), u % PAGE_SLOTS):
                c.wait()


def _attention(q, k, v, bias, page_table, qs, k_new, v_new, logf_new,
               cache_kt, cache_vt, cache_lt):
    b, hp, t, w = q.shape
    n_seq, n_new, d = qs.shape
    groups = page_table.shape[1] // PAGES_PER_STEP
    rows = n_new * N_HEADS
    grid = (b, hp)
    n_sample_steps = n_seq * groups
    assert Q_TILE == 2 * KV_TILE and KV_TILE == ROW_TILE and t % Q_TILE == 0
    assert n_sample_steps == b * hp * SAMPLE_STEPS and groups % SAMPLE_STEPS == 0
    assert SAMPLE_STEPS % PAGE_SLOTS == 0

    def seq_of(i, p):
        return ((i * hp + p) * SAMPLE_STEPS) // groups

    new_spec = pl.BlockSpec((1, n_new, d), lambda i, p, pt: (seq_of(i, p), 0, 0))
    any_spec = pl.BlockSpec(memory_space=pl.ANY)
    grid_spec = pltpu.PrefetchScalarGridSpec(
        num_scalar_prefetch=1,
        grid=grid,
        in_specs=[
            pl.BlockSpec((1, 1, t, w), lambda i, p, pt: (i, p, 0, 0)),
            pl.BlockSpec((1, 1, t, w), lambda i, p, pt: (i, p, 0, 0)),
            pl.BlockSpec((1, 1, t, w), lambda i, p, pt: (i, p, 0, 0)),
            pl.BlockSpec((1, 1, t // KV_TILE, 2, KV_TILE), lambda i, p, pt: (i, p, 0, 0, 0)),
            new_spec, new_spec, new_spec,
            pl.BlockSpec((1, n_new, N_HEADS), lambda i, p, pt: (seq_of(i, p), 0, 0)),
            any_spec, any_spec, any_spec,
        ],
        out_specs=[
            pl.BlockSpec((1, t, w), lambda i, p, pt: (i, 0, p)),
            new_spec,
        ],
        scratch_shapes=[
            pltpu.VMEM((PAGE_SLOTS, PAGES_PER_STEP, d, PAGE), F32),
            pltpu.VMEM((PAGE_SLOTS, PAGES_PER_STEP, d, PAGE), F32),
            pltpu.VMEM((PAGE_SLOTS, PAGES_PER_STEP, N_HEADS, PAGE), F32),
            pltpu.SemaphoreType.DMA((3, PAGE_SLOTS)),
            pltpu.VMEM((rows, d), BF16),
            pltpu.VMEM((PAGE, d), F32),
            pltpu.VMEM((PAGE, d), F32),
            pltpu.VMEM((PAGE, N_HEADS), F32),
            pltpu.VMEM((rows, 1), F32),
            pltpu.VMEM((rows, 1), F32),
            pltpu.VMEM((rows, d), F32),
            pltpu.VMEM((N_HEADS, 128), F32),
        ],
    )
    return pl.pallas_call(
        functools.partial(_attn_body, n_new, groups, n_sample_steps),
        grid_spec=grid_spec,
        out_shape=[jax.ShapeDtypeStruct((b, t, hp * w), BF16),
                   jax.ShapeDtypeStruct((n_seq, n_new, d), F32)],
        compiler_params=pltpu.CompilerParams(
            dimension_semantics=("arbitrary", "arbitrary"),
            vmem_limit_bytes=VMEM_LIMIT),
        name="attention",
    )(page_table, q, k, v, bias, qs, k_new, v_new, logf_new, cache_kt, cache_vt, cache_lt)


def _out_body(o_ref, gate_ref, x1_ref, w_ref, g_ref, y_ref):
    og = (o_ref[...].astype(F32) * gate_ref[...].astype(F32)).astype(BF16)
    x2 = x1_ref[...] + jnp.dot(og, w_ref[...], preferred_element_type=F32)
    y_ref[...] = _rms_norm(x2, g_ref[...])


def _out_proj(o, gate, x1, w, g):
    rows, d = x1.shape
    tile = min(OUT_TILE, rows)
    row_spec = pl.BlockSpec((tile, d), lambda i: (i, 0))
    return pl.pallas_call(
        _out_body,
        grid=(rows // tile,),
        in_specs=[row_spec, row_spec, row_spec, _const_spec(w.shape), _const_spec((1, d))],
        out_specs=row_spec,
        out_shape=jax.ShapeDtypeStruct((rows, d), F32),
        compiler_params=pltpu.CompilerParams(
            dimension_semantics=("parallel",), vmem_limit_bytes=VMEM_LIMIT),
        name="out_proj",
    )(o, gate, x1, w, g)


def kernel(x_prompt, x_sample, state_pool, cache_k, cache_v, cache_logf, page_table,
           norm_g, w_in_pool, pool_mix, pool_scale, w_out_pool,
           w_in_attn, b_forget, w_out_attn, final_norm):
    b, t, d = x_prompt.shape
    n_seq, n_new, _ = x_sample.shape
    n_phys = cache_k.shape[1]
    past = page_table.shape[1] * cache_k.shape[2]
    assert norm_g.shape[0] == 2 and d == D_MODEL and cache_k.shape[2] == PAGE

    g0 = norm_g[0].reshape(1, d)
    g1 = norm_g[1].reshape(1, d)
    gf = final_norm.reshape(1, d)
    win = w_in_pool[0].astype(BF16)
    mix = pool_mix[0].astype(BF16)
    scale = pool_scale[0].reshape(1, POOL_WIDTH)
    wout_pool = w_out_pool[0].astype(BF16)
    w_attn = w_in_attn[0, :, :4 * d].astype(BF16)
    wf = jnp.pad(w_in_attn[0, :, 4 * d:], ((0, 0), (0, 128 - N_HEADS))).astype(BF16)
    bf = jnp.pad(b_forget[0], (0, 128 - N_HEADS)).reshape(1, 128)
    wout_attn = w_out_attn[0].astype(BF16)

    x1p, last = _pool_prompt(x_prompt, g0, win, mix, scale, wout_pool)
    xs_tm = x_sample.transpose(1, 0, 2).reshape(n_new * n_seq, d)
    st_tm = state_pool[0].transpose(1, 0, 2).reshape(POOL_CTX * n_seq, POOL_WIDTH)
    x1s_tm, u_tm = _pool_sample(xs_tm, st_tm, n_seq, n_new, past, g0, win, mix, scale, wout_pool)
    x1s = x1s_tm.reshape(n_new, n_seq, d).transpose(1, 0, 2).reshape(n_seq * n_new, d)
    u_new = u_tm.reshape(n_new, n_seq, POOL_WIDTH).transpose(1, 0, 2)

    qp, kp, vp, k32, v32, gate_p, logf_p, bias_p = _proj_prompt(x1p, g1, w_attn, wf, bf)
    qs, ks, vs, gate_s, logf_s = _proj_sample(x1s, g1, w_attn, wf, bf)
    o_p, o_s = _attention(qp, kp, vp, bias_p, page_table,
                          qs.reshape(n_seq, n_new, d), ks.reshape(n_seq, n_new, d),
                          vs.reshape(n_seq, n_new, d), logf_s.reshape(n_seq, n_new, N_HEADS),
                          cache_k[0].transpose(0, 2, 3, 1).reshape(n_phys, d, PAGE),
                          cache_v[0].transpose(0, 2, 3, 1).reshape(n_phys, d, PAGE),
                          cache_logf[0].transpose(0, 2, 1))
    y_p = _out_proj(o_p.reshape(b * t, d), gate_p.reshape(b * t, d), x1p.reshape(b * t, d),
                    wout_attn, gf).reshape(b, t, d)
    y_s = _out_proj(o_s.reshape(n_seq * n_new, d).astype(BF16), gate_s, x1s, wout_attn, gf)

    pool_prompt = last[:, POOL_HALO - POOL_CTX:, :][None]
    pool_sample = jnp.concatenate([state_pool[0][:, n_new:, :], u_new], axis=1)[None]
    heads = (N_HEADS, HEAD_DIM)
    return (y_p, y_s.reshape(n_seq, n_new, d), pool_prompt, pool_sample,
            k32.reshape(1, b, t, *heads), v32.reshape(1, b, t, *heads), logf_p[None],
            ks.reshape(1, n_seq, n_new, *heads), vs.reshape(1, n_seq, n_new, *heads),
            logf_s.reshape(1, n_seq, n_new, N_HEADS))
```

```python
import functools

import jax
import jax.numpy as jnp
from jax import lax
from jax.experimental import pallas as pl
from jax.experimental.pallas import tpu as pltpu

F32 = jnp.float32
BF16 = jnp.bfloat16

D_MODEL = 1024
POOL_WIDTH = 2048
POOL_WINDOWS = (2, 4, 8, 16)
POOL_GROUP = POOL_WIDTH // len(POOL_WINDOWS)
POOL_CTX = max(POOL_WINDOWS) - 1
POOL_HALO = 16
N_HEADS = 16
HEAD_DIM = 64
HEAD_PAIRS = N_HEADS // 2
PAIR_WIDTH = 2 * HEAD_DIM
RMS_EPS = 1e-6
NEG_BIG = -1e30
LOG2E = 1.4426950408889634

ROW_TILE = 512
OUT_TILE = 1024
Q_TILE = 1024
KV_TILE = 512
PAGE = 128
PAGES_PER_STEP = 8
SAMPLE_STEPS = 4
PAGE_SLOTS = 2

VMEM_LIMIT = 56 * 1024 * 1024


def _rms_norm(x, g):
    ms = jnp.mean(x * x, axis=-1, keepdims=True)
    return x * lax.rsqrt(ms + RMS_EPS) * g


def _silu(z):
    return z * jax.nn.sigmoid(z)


def _log_sigmoid(x):
    return jnp.minimum(x, 0.0) - jnp.log1p(jnp.exp(-jnp.abs(x)))


def _const_spec(shape):
    return pl.BlockSpec(shape, lambda *_: (0,) * len(shape), pipeline_mode=pl.Buffered(1))


def _pool_tail(hb, pooled_fn, win_ref, mix_ref, scale_ref, wout_ref, resid):
    acc = resid
    for g in range(len(POOL_WINDOWS)):
        cols = slice(g * POOL_GROUP, (g + 1) * POOL_GROUP)
        zcols = slice(POOL_WIDTH + g * POOL_GROUP, POOL_WIDTH + (g + 1) * POOL_GROUP)
        u = jnp.dot(hb, win_ref[:, cols], preferred_element_type=F32)
        pooled = pooled_fn(g, u)
        y = jnp.dot(pooled.astype(BF16), mix_ref[g], preferred_element_type=F32)
        y = y * scale_ref[:, cols]
        z = jnp.dot(hb, win_ref[:, zcols], preferred_element_type=F32)
        yg = (y * _silu(z)).astype(BF16)
        acc = acc + jnp.dot(yg, wout_ref[cols, :], preferred_element_type=F32)
    return acc


def _pool_prompt_body(x_ref, g_ref, win_ref, mix_ref, scale_ref, wout_ref,
                      x1_ref, last_ref, halo_ref):
    t = pl.program_id(1)

    @pl.when(t == 0)
    def _():
        halo_ref[...] = jnp.zeros_like(halo_ref)

    x = x_ref[0]
    hb = _rms_norm(x, g_ref[...]).astype(BF16)
    pos = t * ROW_TILE + lax.broadcasted_iota(jnp.int32, (ROW_TILE, 1), 0)

    def pooled_fn(g, u):
        w = POOL_WINDOWS[g]
        cols = slice(g * POOL_GROUP, (g + 1) * POOL_GROUP)
        e = jnp.concatenate([halo_ref[:, cols], u], axis=0)
        tail = u[ROW_TILE - POOL_HALO:, :]
        halo_ref[:, cols] = tail
        last_ref[0, :, cols] = tail
        sh = 1
        while sh < w:
            e = e + pltpu.roll(e, sh, 0)
            sh *= 2
        inv = 1.0 / jnp.minimum(pos + 1, w).astype(F32)
        return e[POOL_HALO:, :] * inv - u

    x1_ref[0] = _pool_tail(hb, pooled_fn, win_ref, mix_ref, scale_ref, wout_ref, x)


def _pool_prompt(x, g, win, mix, scale, wout):
    b, t, d = x.shape
    grid = (b, t // ROW_TILE)
    return pl.pallas_call(
        _pool_prompt_body,
        grid=grid,
        in_specs=[
            pl.BlockSpec((1, ROW_TILE, d), lambda i, j: (i, j, 0)),
            _const_spec((1, d)),
            _const_spec(win.shape),
            _const_spec(mix.shape),
            _const_spec((1, POOL_WIDTH)),
            _const_spec(wout.shape),
        ],
        out_specs=[
            pl.BlockSpec((1, ROW_TILE, d), lambda i, j: (i, j, 0)),
            pl.BlockSpec((1, POOL_HALO, POOL_WIDTH), lambda i, j: (i, 0, 0)),
        ],
        out_shape=[
            jax.ShapeDtypeStruct((b, t, d), F32),
            jax.ShapeDtypeStruct((b, POOL_HALO, POOL_WIDTH), F32),
        ],
        scratch_shapes=[pltpu.VMEM((POOL_HALO, POOL_WIDTH), F32)],
        compiler_params=pltpu.CompilerParams(
            dimension_semantics=("parallel", "arbitrary"),
            vmem_limit_bytes=VMEM_LIMIT),
        name="pool_prompt",
    )(x, g, win, mix, scale, wout)


def _pool_sample_body(n_seq, n_new, start, x_ref, st_ref, g_ref, winu_ref, winz_ref, mix_ref,
                      scale_ref, wout_ref, x1_ref, u_ref, hb_ref, pooled_ref):
    grp = pl.program_id(0)

    @pl.when(grp == 0)
    def _():
        x = x_ref[...]
        hb_ref[...] = _rms_norm(x, g_ref[...]).astype(BF16)
        x1_ref[...] = x

    hb = hb_ref[...]
    u = jnp.dot(hb, winu_ref[...], preferred_element_type=F32)
    u_ref[...] = u

    def slab(e):
        if e < POOL_CTX:
            return st_ref[e * n_seq:(e + 1) * n_seq, :]
        return u[(e - POOL_CTX) * n_seq:(e - POOL_CTX + 1) * n_seq, :]

    def pooled(w):
        outs = []
        for t in range(n_new):
            s = slab(POOL_CTX + t)
            for e in range(POOL_CTX + t - w + 1, POOL_CTX + t):
                s = s + slab(e)
            cnt = min(start + t + 1, w)
            outs.append(s * (1.0 / cnt) - slab(POOL_CTX + t))
        return jnp.concatenate(outs, axis=0)

    for g, w in enumerate(POOL_WINDOWS):
        @pl.when(grp == g)
        def _():
            pooled_ref[...] = pooled(w)

    y = jnp.dot(pooled_ref[...].astype(BF16), mix_ref[0], preferred_element_type=F32)
    y = y * scale_ref[...]
    z = jnp.dot(hb, winz_ref[...], preferred_element_type=F32)
    yg = (y * _silu(z)).astype(BF16)
    x1_ref[...] += jnp.dot(yg, wout_ref[...], preferred_element_type=F32)


def _pool_sample(x_tm, state_tm, n_seq, n_new, start, g, win, mix, scale, wout):
    rows, d = x_tm.shape
    n_groups = len(POOL_WINDOWS)
    return pl.pallas_call(
        functools.partial(_pool_sample_body, n_seq, n_new, start),
        grid=(n_groups,),
        in_specs=[
            pl.BlockSpec((rows, d), lambda i: (0, 0)),
            pl.BlockSpec((state_tm.shape[0], POOL_GROUP), lambda i: (0, i)),
            pl.BlockSpec((1, d), lambda i: (0, 0)),
            pl.BlockSpec((d, POOL_GROUP), lambda i: (0, i)),
            pl.BlockSpec((d, POOL_GROUP), lambda i: (0, n_groups + i)),
            pl.BlockSpec((1, POOL_GROUP, POOL_GROUP), lambda i: (i, 0, 0)),
            pl.BlockSpec((1, POOL_GROUP), lambda i: (0, i)),
            pl.BlockSpec((POOL_GROUP, d), lambda i: (i, 0)),
        ],
        out_specs=[
            pl.BlockSpec((rows, d), lambda i: (0, 0)),
            pl.BlockSpec((rows, POOL_GROUP), lambda i: (0, i)),
        ],
        out_shape=[
            jax.ShapeDtypeStruct((rows, d), F32),
            jax.ShapeDtypeStruct((rows, POOL_WIDTH), F32),
        ],
        scratch_shapes=[pltpu.VMEM((rows, d), BF16), pltpu.VMEM((rows, POOL_GROUP), F32)],
        compiler_params=pltpu.CompilerParams(
            dimension_semantics=("arbitrary",), vmem_limit_bytes=VMEM_LIMIT),
        name="pool_sample",
    )(x_tm, state_tm, g, win, win, mix, scale, wout)


def _project(x, g_ref, w_ref, wf_ref, bf_ref):
    hb = _rms_norm(x, g_ref[...]).astype(BF16)
    parts = [jnp.dot(hb, w_ref[:, i * D_MODEL:(i + 1) * D_MODEL], preferred_element_type=F32)
             for i in range(4)]
    f = jnp.dot(hb, wf_ref[...], preferred_element_type=F32) + bf_ref[...]
    q = parts[0] * (LOG2E * HEAD_DIM ** -0.5)
    return q, parts[1], parts[2], _silu(parts[3]), _log_sigmoid(f)


def _lane_cumsum(x):
    lane = lax.broadcasted_iota(jnp.int32, x.shape, 1)
    sh = 1
    while sh < x.shape[1]:
        x = x + jnp.where(lane >= sh, pltpu.roll(x, sh, 1), 0.0)
        sh *= 2
    return x


def _proj_prompt_body(x_ref, g_ref, w_ref, wf_ref, bf_ref,
                      q_ref, kp_ref, vp_ref, k_ref, v_ref, gate_ref, logf_ref, bias_ref,
                      carry_ref):
    t = pl.program_id(1)

    @pl.when(t == 0)
    def _():
        carry_ref[...] = jnp.zeros_like(carry_ref)

    q, k, v, gate, logf = _project(x_ref[0], g_ref, w_ref, wf_ref, bf_ref)
    k_ref[0] = k
    v_ref[0] = v
    gate_ref[0] = gate.astype(BF16)
    logf_ref[0] = logf[:, :N_HEADS]
    qb, kb, vb = q.astype(BF16), k.astype(BF16), v.astype(BF16)
    for p in range(HEAD_PAIRS):
        cols = slice(p * PAIR_WIDTH, (p + 1) * PAIR_WIDTH)
        q_ref[0, p] = qb[:, cols]
        kp_ref[0, p] = kb[:, cols]
        vp_ref[0, p] = vb[:, cols]

    lt = logf.T[:N_HEADS, :]
    carry = carry_ref[...]
    for c in range(ROW_TILE // 128):
        csum = _lane_cumsum(lt[:, c * 128:(c + 1) * 128]) + carry
        neg = csum * (-LOG2E)
        for p in range(HEAD_PAIRS):
            bias_ref[0, p, 0, :, c * 128:(c + 1) * 128] = neg[2 * p:2 * p + 2, :]
        carry = jnp.broadcast_to(csum[:, 127:128], carry.shape)
    carry_ref[...] = carry


def _proj_prompt(x1, g, w, wf, bf):
    b, t, d = x1.shape
    grid = (b, t // ROW_TILE)
    per_kv = KV_TILE // ROW_TILE
    row_spec = pl.BlockSpec((1, ROW_TILE, d), lambda i, j: (i, j, 0))
    pair_spec = pl.BlockSpec((1, HEAD_PAIRS, ROW_TILE, PAIR_WIDTH), lambda i, j: (i, 0, j, 0))
    pair_shape = jax.ShapeDtypeStruct((b, HEAD_PAIRS, t, PAIR_WIDTH), BF16)
    return pl.pallas_call(
        _proj_prompt_body,
        grid=grid,
        in_specs=[row_spec, _const_spec((1, d)), _const_spec(w.shape),
                  _const_spec(wf.shape), _const_spec(bf.shape)],
        out_specs=[
            pair_spec, pair_spec, pair_spec, row_spec, row_spec, row_spec,
            pl.BlockSpec((1, ROW_TILE, N_HEADS), lambda i, j: (i, j, 0)),
            pl.BlockSpec((1, HEAD_PAIRS, 1, 2, ROW_TILE),
                         lambda i, j: (i, 0, j // per_kv, 0, j % per_kv)),
        ],
        out_shape=[
            pair_shape, pair_shape, pair_shape,
            jax.ShapeDtypeStruct((b, t, d), F32),
            jax.ShapeDtypeStruct((b, t, d), F32),
            jax.ShapeDtypeStruct((b, t, d), BF16),
            jax.ShapeDtypeStruct((b, t, N_HEADS), F32),
            jax.ShapeDtypeStruct((b, HEAD_PAIRS, t // KV_TILE, 2, KV_TILE), F32),
        ],
        scratch_shapes=[pltpu.VMEM((N_HEADS, 128), F32)],
        compiler_params=pltpu.CompilerParams(
            dimension_semantics=("parallel", "arbitrary"),
            vmem_limit_bytes=VMEM_LIMIT),
        name="proj_prompt",
    )(x1, g, w, wf, bf)


def _proj_sample_body(x_ref, g_ref, w_ref, wf_ref, bf_ref,
                      q_ref, k_ref, v_ref, gate_ref, logf_ref, hb_ref):
    i = pl.program_id(0)

    @pl.when(i == 0)
    def _():
        hb = _rms_norm(x_ref[...], g_ref[...]).astype(BF16)
        hb_ref[...] = hb
        f = jnp.dot(hb, wf_ref[...], preferred_element_type=F32) + bf_ref[...]
        logf_ref[...] = _log_sigmoid(f)[:, :N_HEADS]

    part = jnp.dot(hb_ref[...], w_ref[...], preferred_element_type=F32)

    @pl.when(i == 0)
    def _():
        q_ref[...] = (part * (LOG2E * HEAD_DIM ** -0.5)).astype(BF16)

    @pl.when(i == 1)
    def _():
        k_ref[...] = part

    @pl.when(i == 2)
    def _():
        v_ref[...] = part

    @pl.when(i == 3)
    def _():
        gate_ref[...] = _silu(part).astype(BF16)


def _proj_sample(x1, g, w, wf, bf):
    rows, d = x1.shape
    full = lambda shape: pl.BlockSpec(shape, lambda i: (0,) * len(shape))
    return pl.pallas_call(
        _proj_sample_body,
        grid=(4,),
        in_specs=[full((rows, d)), full((1, d)), pl.BlockSpec((d, d), lambda i: (0, i)),
                  full(wf.shape), full(bf.shape)],
        out_specs=[full((rows, d)), full((rows, d)), full((rows, d)), full((rows, d)),
                   full((rows, N_HEADS))],
        out_shape=[
            jax.ShapeDtypeStruct((rows, d), BF16),
            jax.ShapeDtypeStruct((rows, d), F32),
            jax.ShapeDtypeStruct((rows, d), F32),
            jax.ShapeDtypeStruct((rows, d), BF16),
            jax.ShapeDtypeStruct((rows, N_HEADS), F32),
        ],
        scratch_shapes=[pltpu.VMEM((rows, d), BF16)],
        compiler_params=pltpu.CompilerParams(
            dimension_semantics=("arbitrary",), vmem_limit_bytes=VMEM_LIMIT),
        name="proj_sample",
    )(x1, g, w, wf, bf)


def _attn_body(n_new, groups, n_sample_steps, pt_ref,
               q_ref, k_ref, v_ref, bias_ref,
               qs_ref, kn_ref, vn_ref, ln_ref, ck_hbm, cv_hbm, cl_hbm,
               o_ref, os_ref,
               kbuf, vbuf, lbuf, sems, qbd_ref, kpad, vpad, lpad,
               m_ref, l_ref, acc_ref, carry_ref):
    flat = pl.program_id(0) * pl.num_programs(1) + pl.program_id(1)
    rows = n_new * N_HEADS

    def copies(sidx, sl):
        bb = sidx // groups
        gg = sidx % groups
        out = []
        for p in range(PAGES_PER_STEP):
            page = pt_ref[bb, (groups - 1 - gg) * PAGES_PER_STEP + p]
            out.append(pltpu.make_async_copy(ck_hbm.at[page], kbuf.at[sl, p], sems.at[0, sl]))
            out.append(pltpu.make_async_copy(cv_hbm.at[page], vbuf.at[sl, p], sems.at[1, sl]))
            out.append(pltpu.make_async_copy(cl_hbm.at[page], lbuf.at[sl, p], sems.at[2, sl]))
        return out

    row_head = lax.broadcasted_iota(jnp.int32, (N_HEADS, D_MODEL), 0)
    lane_head = lax.broadcasted_iota(jnp.int32, (N_HEADS, D_MODEL), 1) // HEAD_DIM
    own = row_head == lane_head

    def suffix_bias(lp_t, carry):
        incl = _lane_cumsum(lp_t)
        total = jnp.broadcast_to(incl[:, 127:128], incl.shape)
        return total - incl + carry, carry + total

    def accumulate(s, pv_fn):
        m_old = m_ref[...]
        m_new = jnp.maximum(m_old, jnp.max(s, axis=1, keepdims=True))
        alpha = jnp.exp2(m_old - m_new)
        p = jnp.exp2(s - m_new)
        l_ref[...] = alpha * l_ref[...] + jnp.sum(p, axis=1, keepdims=True)
        acc_ref[...] = acc_ref[...] * alpha + pv_fn(p.astype(BF16))
        m_ref[...] = m_new

    def sample_begin():
        q = qs_ref[0].astype(F32)
        for t in range(n_new):
            qt = jnp.broadcast_to(q[t:t + 1, :], (N_HEADS, D_MODEL))
            qbd_ref[t * N_HEADS:(t + 1) * N_HEADS, :] = jnp.where(own, qt, 0.0).astype(BF16)
        m_ref[...] = jnp.full_like(m_ref, NEG_BIG)
        l_ref[...] = jnp.zeros_like(l_ref)
        acc_ref[...] = jnp.zeros_like(acc_ref)
        kpad[0:n_new, :] = kn_ref[0]
        vpad[0:n_new, :] = vn_ref[0]
        lpad[0:n_new, :] = ln_ref[0]
        eye = (lax.broadcasted_iota(jnp.int32, (N_HEADS, N_HEADS), 0) ==
               lax.broadcasted_iota(jnp.int32, (N_HEADS, N_HEADS), 1)).astype(F32)
        lp_t = lax.dot_general(eye, lpad[...], (((1,), (1,)), ((), ())),
                               precision=lax.Precision.HIGHEST, preferred_element_type=F32)
        bias, carry = suffix_bias(lp_t * LOG2E, jnp.zeros((N_HEADS, 128), F32))
        carry_ref[...] = carry
        s = lax.dot_general(qbd_ref[...], kpad[...].astype(BF16), (((1,), (1,)), ((), ())),
                            preferred_element_type=F32)
        s = s + jnp.concatenate([bias] * n_new, axis=0)
        q_idx = lax.broadcasted_iota(jnp.int32, (rows, PAGE), 0) // N_HEADS
        k_idx = lax.broadcasted_iota(jnp.int32, (rows, PAGE), 1)
        s = jnp.where(k_idx <= q_idx, s, NEG_BIG)
        vb = vpad[...].astype(BF16)
        accumulate(s, lambda p: jnp.dot(p, vb, preferred_element_type=F32))

    def sample_pages(slot):
        carry = carry_ref[...]
        biases = [None] * PAGES_PER_STEP
        for p in reversed(range(PAGES_PER_STEP)):
            biases[p], carry = suffix_bias(lbuf[slot, p] * LOG2E, carry)
        carry_ref[...] = carry
        bias = jnp.concatenate(biases, axis=1)
        kcat = jnp.concatenate([kbuf[slot, p].astype(BF16) for p in range(PAGES_PER_STEP)], axis=1)
        s = jnp.dot(qbd_ref[...], kcat, preferred_element_type=F32)
        s = s + jnp.concatenate([bias] * n_new, axis=0)
        vcat = jnp.concatenate([vbuf[slot, p].astype(BF16) for p in range(PAGES_PER_STEP)], axis=1)
        accumulate(s, lambda p: lax.dot_general(p, vcat, (((1,), (1,)), ((), ())),
                                                preferred_element_type=F32))

    def sample_end():
        acc = acc_ref[...] * (1.0 / l_ref[...])
        for t in range(n_new):
            blk = jnp.where(own, acc[t * N_HEADS:(t + 1) * N_HEADS, :], 0.0)
            os_ref[0, t:t + 1, :] = jnp.sum(blk, axis=0, keepdims=True)

    @pl.when(flat == 0)
    def _():
        for u in range(PAGE_SLOTS):
            for c in copies(u, u):
                c.start()
        kpad[...] = jnp.zeros_like(kpad)
        vpad[...] = jnp.zeros_like(vpad)
        lpad[...] = jnp.zeros_like(lpad)

    first_sidx = flat * SAMPLE_STEPS

    def refill_of(u):
        return (first_sidx + u + PAGE_SLOTS) % n_sample_steps

    def sample_step(u):
        slot = u % PAGE_SLOTS
        for c in copies(first_sidx + u, slot):
            c.wait()
        sample_pages(slot)
        for c in copies(refill_of(u), slot):
            c.start()

    lane = lax.broadcasted_iota(jnp.int32, (1, PAIR_WIDTH), 1)
    first = lane < HEAD_DIM
    half = Q_TILE // 2
    causal = (lax.broadcasted_iota(jnp.int32, (half, KV_TILE), 0) >=
              lax.broadcasted_iota(jnp.int32, (half, KV_TILE), 1))

    def query_heads(qi):
        q = q_ref[0, 0, qi * Q_TILE:(qi + 1) * Q_TILE, :]
        zero = jnp.zeros_like(q)
        return jnp.where(first, q, zero), jnp.where(first, zero, q)

    def head_update(qh, kt, vt, brow, m, l, mask_rows):
        s = lax.dot_general(qh, kt, (((1,), (1,)), ((), ())), preferred_element_type=F32)
        s = s + brow
        if mask_rows is not None:
            top = jnp.where(causal, s[:mask_rows], NEG_BIG)
            s = top if mask_rows == s.shape[0] else jnp.concatenate([top, s[mask_rows:]], axis=0)
        m_new = jnp.maximum(m, jnp.max(s, axis=1, keepdims=True))
        alpha = jnp.exp2(m - m_new)
        p = jnp.exp2(s - m_new)
        l_new = alpha * l + jnp.sum(p, axis=1, keepdims=True)
        pv = jnp.dot(p.astype(BF16), vt, preferred_element_type=F32)
        return m_new, l_new, alpha, pv

    def key_tile(q_heads, t, carry, rows_, mask_rows):
        m0, l0, m1, l1, acc = carry
        kt = k_ref[0, 0, t * KV_TILE:(t + 1) * KV_TILE, :]
        vt = v_ref[0, 0, t * KV_TILE:(t + 1) * KV_TILE, :]
        bt = bias_ref[0, 0, t]
        m0, l0, a0, pv0 = head_update(q_heads[0][rows_], kt, vt, bt[0:1, :], m0, l0, mask_rows)
        m1, l1, a1, pv1 = head_update(q_heads[1][rows_], kt, vt, bt[1:2, :], m1, l1, mask_rows)
        acc = acc * jnp.where(first, a0, a1) + jnp.where(first, pv0, pv1)
        return m0, l0, m1, l1, acc

    everything = slice(0, Q_TILE)
    neg = jnp.full((Q_TILE, 1), NEG_BIG, F32)
    zcol = jnp.zeros((Q_TILE, 1), F32)
    init = (neg, zcol, neg, zcol, jnp.zeros((Q_TILE, PAIR_WIDTH), F32))

    def visible_pair(q_heads, i, carry):
        carry = key_tile(q_heads, 2 * i, carry, everything, None)
        return key_tile(q_heads, 2 * i + 1, carry, everything, None)

    def diagonal_block(q_heads, qi, carry):
        m0, l0, m1, l1, acc = key_tile(q_heads, 2 * qi, carry, everything, half)
        lower = slice(half, Q_TILE)
        low = key_tile(q_heads, 2 * qi + 1,
                       (m0[lower], l0[lower], m1[lower], l1[lower], acc[lower]), lower, half)
        base = qi * Q_TILE
        o_ref[0, base:base + half, :] = (
            acc[:half] * jnp.where(first, 1.0 / l0[:half], 1.0 / l1[:half])).astype(BF16)
        o_ref[0, base + half:base + Q_TILE, :] = (
            low[4] * jnp.where(first, 1.0 / low[1], 1.0 / low[3])).astype(BF16)

    n_q = q_ref.shape[2] // Q_TILE
    phases = []
    for qi in range(n_q):
        for i in range(qi):
            phases.append(("pair", qi, i))
        phases.append(("diag", qi, None))
    assert len(phases) + 1 == SAMPLE_STEPS

    pl.when(first_sidx % groups == 0)(sample_begin)
    sample_step(0)
    q_heads, carry = None, None
    for n, (kind, qi, i) in enumerate(phases):
        if kind == "diag" and qi == 0 or kind == "pair" and i == 0:
            q_heads, carry = query_heads(qi), init
        if kind == "pair":
            carry = visible_pair(q_heads, i, carry)
        else:
            diagonal_block(q_heads, qi, carry)
        sample_step(n + 1)
    pl.when((first_sidx + SAMPLE_STEPS) % groups == 0)(sample_end)

    @pl.when(first_sidx + SAMPLE_STEPS == n_sample_steps)
    def _():
        for u in range(SAMPLE_STEPS - PAGE_SLOTS, SAMPLE_STEPS):
            for c in copies(refill_of(u), u % PAGE_SLOTS):
                c.wait()


def _attention(q, k, v, bias, page_table, qs, k_new, v_new, logf_new,
               cache_kt, cache_vt, cache_lt):
    b, hp, t, w = q.shape
    n_seq, n_new, d = qs.shape
    groups = page_table.shape[1] // PAGES_PER_STEP
    rows = n_new * N_HEADS
    grid = (b, hp)
    n_sample_steps = n_seq * groups
    assert Q_TILE == 2 * KV_TILE and KV_TILE == ROW_TILE and t % Q_TILE == 0
    assert n_sample_steps == b * hp * SAMPLE_STEPS and groups % SAMPLE_STEPS == 0
    assert SAMPLE_STEPS % PAGE_SLOTS == 0

    def seq_of(i, p):
        return ((i * hp + p) * SAMPLE_STEPS) // groups

    new_spec = pl.BlockSpec((1, n_new, d), lambda i, p, pt: (seq_of(i, p), 0, 0))
    any_spec = pl.BlockSpec(memory_space=pl.ANY)
    grid_spec = pltpu.PrefetchScalarGridSpec(
        num_scalar_prefetch=1,
        grid=grid,
        in_specs=[
            pl.BlockSpec((1, 1, t, w), lambda i, p, pt: (i, p, 0, 0)),
            pl.BlockSpec((1, 1, t, w), lambda i, p, pt: (i, p, 0, 0)),
            pl.BlockSpec((1, 1, t, w), lambda i, p, pt: (i, p, 0, 0)),
            pl.BlockSpec((1, 1, t // KV_TILE, 2, KV_TILE), lambda i, p, pt: (i, p, 0, 0, 0)),
            new_spec, new_spec, new_spec,
            pl.BlockSpec((1, n_new, N_HEADS), lambda i, p, pt: (seq_of(i, p), 0, 0)),
            any_spec, any_spec, any_spec,
        ],
        out_specs=[
            pl.BlockSpec((1, t, w), lambda i, p, pt: (i, 0, p)),
            new_spec,
        ],
        scratch_shapes=[
            pltpu.VMEM((PAGE_SLOTS, PAGES_PER_STEP, d, PAGE), F32),
            pltpu.VMEM((PAGE_SLOTS, PAGES_PER_STEP, d, PAGE), F32),
            pltpu.VMEM((PAGE_SLOTS, PAGES_PER_STEP, N_HEADS, PAGE), F32),
            pltpu.SemaphoreType.DMA((3, PAGE_SLOTS)),
            pltpu.VMEM((rows, d), BF16),
            pltpu.VMEM((PAGE, d), F32),
            pltpu.VMEM((PAGE, d), F32),
            pltpu.VMEM((PAGE, N_HEADS), F32),
            pltpu.VMEM((rows, 1), F32),
            pltpu.VMEM((rows, 1), F32),
            pltpu.VMEM((rows, d), F32),
            pltpu.VMEM((N_HEADS, 128), F32),
        ],
    )
    return pl.pallas_call(
        functools.partial(_attn_body, n_new, groups, n_sample_steps),
        grid_spec=grid_spec,
        out_shape=[jax.ShapeDtypeStruct((b, t, hp * w), BF16),
                   jax.ShapeDtypeStruct((n_seq, n_new, d), F32)],
        compiler_params=pltpu.CompilerParams(
            dimension_semantics=("arbitrary", "arbitrary"),
            vmem_limit_bytes=VMEM_LIMIT),
        name="attention",
    )(page_table, q, k, v, bias, qs, k_new, v_new, logf_new, cache_kt, cache_vt, cache_lt)


def _out_body(o_ref, gate_ref, x1_ref, w_ref, g_ref, y_ref):
    og = (o_ref[...].astype(F32) * gate_ref[...].astype(F32)).astype(BF16)
    x2 = x1_ref[...] + jnp.dot(og, w_ref[...], preferred_element_type=F32)
    y_ref[...] = _rms_norm(x2, g_ref[...])


def _out_proj(o, gate, x1, w, g):
    rows, d = x1.shape
    tile = min(OUT_TILE, rows)
    row_spec = pl.BlockSpec((tile, d), lambda i: (i, 0))
    return pl.pallas_call(
        _out_body,
        grid=(rows // tile,),
        in_specs=[row_spec, row_spec, row_spec, _const_spec(w.shape), _const_spec((1, d))],
        out_specs=row_spec,
        out_shape=jax.ShapeDtypeStruct((rows, d), F32),
        compiler_params=pltpu.CompilerParams(
            dimension_semantics=("parallel",), vmem_limit_bytes=VMEM_LIMIT),
        name="out_proj",
    )(o, gate, x1, w, g)


def kernel(x_prompt, x_sample, state_pool, cache_k, cache_v, cache_logf, page_table,
           norm_g, w_in_pool, pool_mix, pool_scale, w_out_pool,
           w_in_attn, b_forget, w_out_attn, final_norm):
    b, t, d = x_prompt.shape
    n_seq, n_new, _ = x_sample.shape
    n_phys = cache_k.shape[1]
    past = page_table.shape[1] * cache_k.shape[2]
    assert norm_g.shape[0] == 2 and d == D_MODEL and cache_k.shape[2] == PAGE

    g0 = norm_g[0].reshape(1, d)
    g1 = norm_g[1].reshape(1, d)
    gf = final_norm.reshape(1, d)
    win = w_in_pool[0].astype(BF16)
    mix = pool_mix[0].astype(BF16)
    scale = pool_scale[0].reshape(1, POOL_WIDTH)
    wout_pool = w_out_pool[0].astype(BF16)
    w_attn = w_in_attn[0, :, :4 * d].astype(BF16)
    wf = jnp.pad(w_in_attn[0, :, 4 * d:], ((0, 0), (0, 128 - N_HEADS))).astype(BF16)
    bf = jnp.pad(b_forget[0], (0, 128 - N_HEADS)).reshape(1, 128)
    wout_attn = w_out_attn[0].astype(BF16)

    x1p, last = _pool_prompt(x_prompt, g0, win, mix, scale, wout_pool)
    xs_tm = x_sample.transpose(1, 0, 2).reshape(n_new * n_seq, d)
    st_tm = state_pool[0].transpose(1, 0, 2).reshape(POOL_CTX * n_seq, POOL_WIDTH)
    x1s_tm, u_tm = _pool_sample(xs_tm, st_tm, n_seq, n_new, past, g0, win, mix, scale, wout_pool)
    x1s = x1s_tm.reshape(n_new, n_seq, d).transpose(1, 0, 2).reshape(n_seq * n_new, d)
    u_new = u_tm.reshape(n_new, n_seq, POOL_WIDTH).transpose(1, 0, 2)

    qp, kp, vp, k32, v32, gate_p, logf_p, bias_p = _proj_prompt(x1p, g1, w_attn, wf, bf)
    qs, ks, vs, gate_s, logf_s = _proj_sample(x1s, g1, w_attn, wf, bf)
    o_p, o_s = _attention(qp, kp, vp, bias_p, page_table,
                          qs.reshape(n_seq, n_new, d), ks.reshape(n_seq, n_new, d),
                          vs.reshape(n_seq, n_new, d), logf_s.reshape(n_seq, n_new, N_HEADS),
                          cache_k[0].transpose(0, 2, 3, 1).reshape(n_phys, d, PAGE),
                          cache_v[0].transpose(0, 2, 3, 1).reshape(n_phys, d, PAGE),
                          cache_logf[0].transpose(0, 2, 1))
    y_p = _out_proj(o_p.reshape(b * t, d), gate_p.reshape(b * t, d), x1p.reshape(b * t, d),
                    wout_attn, gf).reshape(b, t, d)
    y_s = _out_proj(o_s.reshape(n_seq * n_new, d).astype(BF16), gate_s, x1s, wout_attn, gf)

    pool_prompt = last[:, POOL_HALO - POOL_CTX:, :][None]
    pool_sample = jnp.concatenate([state_pool[0][:, n_new:, :], u_new], axis=1)[None]
    heads = (N_HEADS, HEAD_DIM)
    return (y_p, y_s.reshape(n_seq, n_new, d), pool_prompt, pool_sample,
            k32.reshape(1, b, t, *heads), v32.reshape(1, b, t, *heads), logf_p[None],
            ks.reshape(1, n_seq, n_new, *heads), vs.reshape(1, n_seq, n_new, *heads),
            logf_s.reshape(1, n_seq, n_new, N_HEADS))
```

```python
import functools

import jax
import jax.numpy as jnp
from jax import lax
from jax.experimental import pallas as pl
from jax.experimental.pallas import tpu as pltpu

F32 = jnp.float32
BF16 = jnp.bfloat16

D_MODEL = 1024
POOL_WIDTH = 2048
POOL_WINDOWS = (2, 4, 8, 16)
POOL_GROUP = POOL_WIDTH // len(POOL_WINDOWS)
POOL_CTX = max(POOL_WINDOWS) - 1
POOL_HALO = 16
N_HEADS = 16
HEAD_DIM = 64
HEAD_PAIRS = N_HEADS // 2
PAIR_WIDTH = 2 * HEAD_DIM
RMS_EPS = 1e-6
NEG_BIG = -1e30
LOG2E = 1.4426950408889634

ROW_TILE = 512
OUT_TILE = 1024
Q_TILE = 1024
KV_TILE = 512
PAGE = 128
PAGES_PER_STEP = 8
SAMPLE_STEPS = 4
PAGE_SLOTS = 4

VMEM_LIMIT = 56 * 1024 * 1024


def _rms_norm(x, g):
    ms = jnp.mean(x * x, axis=-1, keepdims=True)
    return x * lax.rsqrt(ms + RMS_EPS) * g


def _silu(z):
    return z * jax.nn.sigmoid(z)


def _log_sigmoid(x):
    return jnp.minimum(x, 0.0) - jnp.log1p(jnp.exp(-jnp.abs(x)))


def _const_spec(shape):
    return pl.BlockSpec(shape, lambda *_: (0,) * len(shape), pipeline_mode=pl.Buffered(1))


def _pool_tail(hb, pooled_fn, win_ref, mix_ref, scale_ref, wout_ref, resid):
    acc = resid
    for g in range(len(POOL_WINDOWS)):
        cols = slice(g * POOL_GROUP, (g + 1) * POOL_GROUP)
        zcols = slice(POOL_WIDTH + g * POOL_GROUP, POOL_WIDTH + (g + 1) * POOL_GROUP)
        u = jnp.dot(hb, win_ref[:, cols], preferred_element_type=F32)
        pooled = pooled_fn(g, u)
        y = jnp.dot(pooled.astype(BF16), mix_ref[g], preferred_element_type=F32)
        y = y * scale_ref[:, cols]
        z = jnp.dot(hb, win_ref[:, zcols], preferred_element_type=F32)
        yg = (y * _silu(z)).astype(BF16)
        acc = acc + jnp.dot(yg, wout_ref[cols, :], preferred_element_type=F32)
    return acc


def _pool_prompt_body(x_ref, g_ref, win_ref, mix_ref, scale_ref, wout_ref,
                      x1_ref, last_ref, halo_ref):
    t = pl.program_id(1)

    @pl.when(t == 0)
    def _():
        halo_ref[...] = jnp.zeros_like(halo_ref)

    x = x_ref[0]
    hb = _rms_norm(x, g_ref[...]).astype(BF16)
    pos = t * ROW_TILE + lax.broadcasted_iota(jnp.int32, (ROW_TILE, 1), 0)

    def pooled_fn(g, u):
        w = POOL_WINDOWS[g]
        cols = slice(g * POOL_GROUP, (g + 1) * POOL_GROUP)
        e = jnp.concatenate([halo_ref[:, cols], u], axis=0)
        tail = u[ROW_TILE - POOL_HALO:, :]
        halo_ref[:, cols] = tail
        last_ref[0, :, cols] = tail
        sh = 1
        while sh < w:
            e = e + pltpu.roll(e, sh, 0)
            sh *= 2
        inv = 1.0 / jnp.minimum(pos + 1, w).astype(F32)
        return e[POOL_HALO:, :] * inv - u

    x1_ref[0] = _pool_tail(hb, pooled_fn, win_ref, mix_ref, scale_ref, wout_ref, x)


def _pool_prompt(x, g, win, mix, scale, wout):
    b, t, d = x.shape
    grid = (b, t // ROW_TILE)
    return pl.pallas_call(
        _pool_prompt_body,
        grid=grid,
        in_specs=[
            pl.BlockSpec((1, ROW_TILE, d), lambda i, j: (i, j, 0)),
            _const_spec((1, d)),
            _const_spec(win.shape),
            _const_spec(mix.shape),
            _const_spec((1, POOL_WIDTH)),
            _const_spec(wout.shape),
        ],
        out_specs=[
            pl.BlockSpec((1, ROW_TILE, d), lambda i, j: (i, j, 0)),
            pl.BlockSpec((1, POOL_HALO, POOL_WIDTH), lambda i, j: (i, 0, 0)),
        ],
        out_shape=[
            jax.ShapeDtypeStruct((b, t, d), F32),
            jax.ShapeDtypeStruct((b, POOL_HALO, POOL_WIDTH), F32),
        ],
        scratch_shapes=[pltpu.VMEM((POOL_HALO, POOL_WIDTH), F32)],
        compiler_params=pltpu.CompilerParams(
            dimension_semantics=("parallel", "arbitrary"),
            vmem_limit_bytes=VMEM_LIMIT),
        name="pool_prompt",
    )(x, g, win, mix, scale, wout)


def _pool_sample_body(n_seq, n_new, start, x_ref, st_ref, g_ref, winu_ref, winz_ref, mix_ref,
                      scale_ref, wout_ref, x1_ref, u_ref, hb_ref, pooled_ref):
    grp = pl.program_id(0)

    @pl.when(grp == 0)
    def _():
        x = x_ref[...]
        hb_ref[...] = _rms_norm(x, g_ref[...]).astype(BF16)
        x1_ref[...] = x

    hb = hb_ref[...]
    u = jnp.dot(hb, winu_ref[...], preferred_element_type=F32)
    u_ref[...] = u

    def slab(e):
        if e < POOL_CTX:
            return st_ref[e * n_seq:(e + 1) * n_seq, :]
        return u[(e - POOL_CTX) * n_seq:(e - POOL_CTX + 1) * n_seq, :]

    def pooled(w):
        outs = []
        for t in range(n_new):
            s = slab(POOL_CTX + t)
            for e in range(POOL_CTX + t - w + 1, POOL_CTX + t):
                s = s + slab(e)
            cnt = min(start + t + 1, w)
            outs.append(s * (1.0 / cnt) - slab(POOL_CTX + t))
        return jnp.concatenate(outs, axis=0)

    for g, w in enumerate(POOL_WINDOWS):
        @pl.when(grp == g)
        def _():
            pooled_ref[...] = pooled(w)

    y = jnp.dot(pooled_ref[...].astype(BF16), mix_ref[0], preferred_element_type=F32)
    y = y * scale_ref[...]
    z = jnp.dot(hb, winz_ref[...], preferred_element_type=F32)
    yg = (y * _silu(z)).astype(BF16)
    x1_ref[...] += jnp.dot(yg, wout_ref[...], preferred_element_type=F32)


def _pool_sample(x_tm, state_tm, n_seq, n_new, start, g, win, mix, scale, wout):
    rows, d = x_tm.shape
    n_groups = len(POOL_WINDOWS)
    return pl.pallas_call(
        functools.partial(_pool_sample_body, n_seq, n_new, start),
        grid=(n_groups,),
        in_specs=[
            pl.BlockSpec((rows, d), lambda i: (0, 0)),
            pl.BlockSpec((state_tm.shape[0], POOL_GROUP), lambda i: (0, i)),
            pl.BlockSpec((1, d), lambda i: (0, 0)),
            pl.BlockSpec((d, POOL_GROUP), lambda i: (0, i)),
            pl.BlockSpec((d, POOL_GROUP), lambda i: (0, n_groups + i)),
            pl.BlockSpec((1, POOL_GROUP, POOL_GROUP), lambda i: (i, 0, 0)),
            pl.BlockSpec((1, POOL_GROUP), lambda i: (0, i)),
            pl.BlockSpec((POOL_GROUP, d), lambda i: (i, 0)),
        ],
        out_specs=[
            pl.BlockSpec((rows, d), lambda i: (0, 0)),
            pl.BlockSpec((rows, POOL_GROUP), lambda i: (0, i)),
        ],
        out_shape=[
            jax.ShapeDtypeStruct((rows, d), F32),
            jax.ShapeDtypeStruct((rows, POOL_WIDTH), F32),
        ],
        scratch_shapes=[pltpu.VMEM((rows, d), BF16), pltpu.VMEM((rows, POOL_GROUP), F32)],
        compiler_params=pltpu.CompilerParams(
            dimension_semantics=("arbitrary",), vmem_limit_bytes=VMEM_LIMIT),
        name="pool_sample",
    )(x_tm, state_tm, g, win, win, mix, scale, wout)


def _project(x, g_ref, w_ref, wf_ref, bf_ref):
    hb = _rms_norm(x, g_ref[...]).astype(BF16)
    parts = [jnp.dot(hb, w_ref[:, i * D_MODEL:(i + 1) * D_MODEL], preferred_element_type=F32)
             for i in range(4)]
    f = jnp.dot(hb, wf_ref[...], preferred_element_type=F32) + bf_ref[...]
    q = parts[0] * (LOG2E * HEAD_DIM ** -0.5)
    return q, parts[1], parts[2], _silu(parts[3]), _log_sigmoid(f)


def _lane_cumsum(x):
    lane = lax.broadcasted_iota(jnp.int32, x.shape, 1)
    sh = 1
    while sh < x.shape[1]:
        x = x + jnp.where(lane >= sh, pltpu.roll(x, sh, 1), 0.0)
        sh *= 2
    return x


def _proj_prompt_body(x_ref, g_ref, w_ref, wf_ref, bf_ref,
                      q_ref, kp_ref, vp_ref, k_ref, v_ref, gate_ref, logf_ref, bias_ref,
                      carry_ref):
    t = pl.program_id(1)

    @pl.when(t == 0)
    def _():
        carry_ref[...] = jnp.zeros_like(carry_ref)

    q, k, v, gate, logf = _project(x_ref[0], g_ref, w_ref, wf_ref, bf_ref)
    k_ref[0] = k
    v_ref[0] = v
    gate_ref[0] = gate.astype(BF16)
    logf_ref[0] = logf[:, :N_HEADS]
    qb, kb, vb = q.astype(BF16), k.astype(BF16), v.astype(BF16)
    for p in range(HEAD_PAIRS):
        cols = slice(p * PAIR_WIDTH, (p + 1) * PAIR_WIDTH)
        q_ref[0, p] = qb[:, cols]
        kp_ref[0, p] = kb[:, cols]
        vp_ref[0, p] = vb[:, cols]

    lt = logf.T[:N_HEADS, :]
    carry = carry_ref[...]
    for c in range(ROW_TILE // 128):
        csum = _lane_cumsum(lt[:, c * 128:(c + 1) * 128]) + carry
        neg = csum * (-LOG2E)
        for p in range(HEAD_PAIRS):
            bias_ref[0, p, 0, :, c * 128:(c + 1) * 128] = neg[2 * p:2 * p + 2, :]
        carry = jnp.broadcast_to(csum[:, 127:128], carry.shape)
    carry_ref[...] = carry


def _proj_prompt(x1, g, w, wf, bf):
    b, t, d = x1.shape
    grid = (b, t // ROW_TILE)
    per_kv = KV_TILE // ROW_TILE
    row_spec = pl.BlockSpec((1, ROW_TILE, d), lambda i, j: (i, j, 0))
    pair_spec = pl.BlockSpec((1, HEAD_PAIRS, ROW_TILE, PAIR_WIDTH), lambda i, j: (i, 0, j, 0))
    pair_shape = jax.ShapeDtypeStruct((b, HEAD_PAIRS, t, PAIR_WIDTH), BF16)
    return pl.pallas_call(
        _proj_prompt_body,
        grid=grid,
        in_specs=[row_spec, _const_spec((1, d)), _const_spec(w.shape),
                  _const_spec(wf.shape), _const_spec(bf.shape)],
        out_specs=[
            pair_spec, pair_spec, pair_spec, row_spec, row_spec, row_spec,
            pl.BlockSpec((1, ROW_TILE, N_HEADS), lambda i, j: (i, j, 0)),
            pl.BlockSpec((1, HEAD_PAIRS, 1, 2, ROW_TILE),
                         lambda i, j: (i, 0, j // per_kv, 0, j % per_kv)),
        ],
        out_shape=[
            pair_shape, pair_shape, pair_shape,
            jax.ShapeDtypeStruct((b, t, d), F32),
            jax.ShapeDtypeStruct((b, t, d), F32),
            jax.ShapeDtypeStruct((b, t, d), BF16),
            jax.ShapeDtypeStruct((b, t, N_HEADS), F32),
            jax.ShapeDtypeStruct((b, HEAD_PAIRS, t // KV_TILE, 2, KV_TILE), F32),
        ],
        scratch_shapes=[pltpu.VMEM((N_HEADS, 128), F32)],
        compiler_params=pltpu.CompilerParams(
            dimension_semantics=("parallel", "arbitrary"),
            vmem_limit_bytes=VMEM_LIMIT),
        name="proj_prompt",
    )(x1, g, w, wf, bf)


def _proj_sample_body(x_ref, g_ref, w_ref, wf_ref, bf_ref,
                      q_ref, k_ref, v_ref, gate_ref, logf_ref, hb_ref):
    i = pl.program_id(0)

    @pl.when(i == 0)
    def _():
        hb = _rms_norm(x_ref[...], g_ref[...]).astype(BF16)
        hb_ref[...] = hb
        f = jnp.dot(hb, wf_ref[...], preferred_element_type=F32) + bf_ref[...]
        logf_ref[...] = _log_sigmoid(f)[:, :N_HEADS]

    part = jnp.dot(hb_ref[...], w_ref[...], preferred_element_type=F32)

    @pl.when(i == 0)
    def _():
        q_ref[...] = (part * (LOG2E * HEAD_DIM ** -0.5)).astype(BF16)

    @pl.when(i == 1)
    def _():
        k_ref[...] = part

    @pl.when(i == 2)
    def _():
        v_ref[...] = part

    @pl.when(i == 3)
    def _():
        gate_ref[...] = _silu(part).astype(BF16)


def _proj_sample(x1, g, w, wf, bf):
    rows, d = x1.shape
    full = lambda shape: pl.BlockSpec(shape, lambda i: (0,) * len(shape))
    return pl.pallas_call(
        _proj_sample_body,
        grid=(4,),
        in_specs=[full((rows, d)), full((1, d)), pl.BlockSpec((d, d), lambda i: (0, i)),
                  full(wf.shape), full(bf.shape)],
        out_specs=[full((rows, d)), full((rows, d)), full((rows, d)), full((rows, d)),
                   full((rows, N_HEADS))],
        out_shape=[
            jax.ShapeDtypeStruct((rows, d), BF16),
            jax.ShapeDtypeStruct((rows, d), F32),
            jax.ShapeDtypeStruct((rows, d), F32),
            jax.ShapeDtypeStruct((rows, d), BF16),
            jax.ShapeDtypeStruct((rows, N_HEADS), F32),
        ],
        scratch_shapes=[pltpu.VMEM((rows, d), BF16)],
        compiler_params=pltpu.CompilerParams(
            dimension_semantics=("arbitrary",), vmem_limit_bytes=VMEM_LIMIT),
        name="proj_sample",
    )(x1, g, w, wf, bf)


def _attn_body(n_new, groups, n_sample_steps, pt_ref,
               q_ref, k_ref, v_ref, bias_ref,
               qs_ref, kn_ref, vn_ref, ln_ref, ck_hbm, cv_hbm, cl_hbm,
               o_ref, os_ref,
               kbuf, vbuf, lbuf, sems, qbd_ref, kpad, vpad, lpad,
               m_ref, l_ref, acc_ref, carry_ref):
    flat = pl.program_id(0) * pl.num_programs(1) + pl.program_id(1)
    rows = n_new * N_HEADS

    def copies(sidx, sl):
        bb = sidx // groups
        gg = sidx % groups
        out = []
        for p in range(PAGES_PER_STEP):
            page = pt_ref[bb, (groups - 1 - gg) * PAGES_PER_STEP + p]
            out.append(pltpu.make_async_copy(ck_hbm.at[page], kbuf.at[sl, p], sems.at[0, sl]))
            out.append(pltpu.make_async_copy(cv_hbm.at[page], vbuf.at[sl, p], sems.at[1, sl]))
            out.append(pltpu.make_async_copy(cl_hbm.at[page], lbuf.at[sl, p], sems.at[2, sl]))
        return out

    row_head = lax.broadcasted_iota(jnp.int32, (N_HEADS, D_MODEL), 0)
    lane_head = lax.broadcasted_iota(jnp.int32, (N_HEADS, D_MODEL), 1) // HEAD_DIM
    own = row_head == lane_head

    def suffix_bias(lp_t, carry):
        incl = _lane_cumsum(lp_t)
        total = jnp.broadcast_to(incl[:, 127:128], incl.shape)
        return total - incl + carry, carry + total

    def accumulate(s, pv_fn):
        m_old = m_ref[...]
        m_new = jnp.maximum(m_old, jnp.max(s, axis=1, keepdims=True))
        alpha = jnp.exp2(m_old - m_new)
        p = jnp.exp2(s - m_new)
        l_ref[...] = alpha * l_ref[...] + jnp.sum(p, axis=1, keepdims=True)
        acc_ref[...] = acc_ref[...] * alpha + pv_fn(p.astype(BF16))
        m_ref[...] = m_new

    def sample_begin():
        q = qs_ref[0].astype(F32)
        for t in range(n_new):
            qt = jnp.broadcast_to(q[t:t + 1, :], (N_HEADS, D_MODEL))
            qbd_ref[t * N_HEADS:(t + 1) * N_HEADS, :] = jnp.where(own, qt, 0.0).astype(BF16)
        m_ref[...] = jnp.full_like(m_ref, NEG_BIG)
        l_ref[...] = jnp.zeros_like(l_ref)
        acc_ref[...] = jnp.zeros_like(acc_ref)
        kpad[0:n_new, :] = kn_ref[0]
        vpad[0:n_new, :] = vn_ref[0]
        lpad[0:n_new, :] = ln_ref[0]
        eye = (lax.broadcasted_iota(jnp.int32, (N_HEADS, N_HEADS), 0) ==
               lax.broadcasted_iota(jnp.int32, (N_HEADS, N_HEADS), 1)).astype(F32)
        lp_t = lax.dot_general(eye, lpad[...], (((1,), (1,)), ((), ())),
                               precision=lax.Precision.HIGHEST, preferred_element_type=F32)
        bias, carry = suffix_bias(lp_t * LOG2E, jnp.zeros((N_HEADS, 128), F32))
        carry_ref[...] = carry
        s = lax.dot_general(qbd_ref[...], kpad[...].astype(BF16), (((1,), (1,)), ((), ())),
                            preferred_element_type=F32)
        s = s + jnp.concatenate([bias] * n_new, axis=0)
        q_idx = lax.broadcasted_iota(jnp.int32, (rows, PAGE), 0) // N_HEADS
        k_idx = lax.broadcasted_iota(jnp.int32, (rows, PAGE), 1)
        s = jnp.where(k_idx <= q_idx, s, NEG_BIG)
        vb = vpad[...].astype(BF16)
        accumulate(s, lambda p: jnp.dot(p, vb, preferred_element_type=F32))

    def sample_pages(slot):
        carry = carry_ref[...]
        biases = [None] * PAGES_PER_STEP
        for p in reversed(range(PAGES_PER_STEP)):
            biases[p], carry = suffix_bias(lbuf[slot, p] * LOG2E, carry)
        carry_ref[...] = carry
        bias = jnp.concatenate(biases, axis=1)
        kcat = jnp.concatenate([kbuf[slot, p].astype(BF16) for p in range(PAGES_PER_STEP)], axis=1)
        s = jnp.dot(qbd_ref[...], kcat, preferred_element_type=F32)
        s = s + jnp.concatenate([bias] * n_new, axis=0)
        vcat = jnp.concatenate([vbuf[slot, p].astype(BF16) for p in range(PAGES_PER_STEP)], axis=1)
        accumulate(s, lambda p: lax.dot_general(p, vcat, (((1,), (1,)), ((), ())),
                                                preferred_element_type=F32))

    def sample_end():
        acc = acc_ref[...] * (1.0 / l_ref[...])
        for t in range(n_new):
            blk = jnp.where(own, acc[t * N_HEADS:(t + 1) * N_HEADS, :], 0.0)
            os_ref[0, t:t + 1, :] = jnp.sum(blk, axis=0, keepdims=True)

    @pl.when(flat == 0)
    def _():
        for u in range(PAGE_SLOTS):
            for c in copies(u, u):
                c.start()
        kpad[...] = jnp.zeros_like(kpad)
        vpad[...] = jnp.zeros_like(vpad)
        lpad[...] = jnp.zeros_like(lpad)

    first_sidx = flat * SAMPLE_STEPS

    def refill_of(u):
        return (first_sidx + u + PAGE_SLOTS) % n_sample_steps

    def sample_step(u):
        slot = u % PAGE_SLOTS
        for c in copies(first_sidx + u, slot):
            c.wait()
        sample_pages(slot)
        for c in copies(refill_of(u), slot):
            c.start()

    lane = lax.broadcasted_iota(jnp.int32, (1, PAIR_WIDTH), 1)
    first = lane < HEAD_DIM
    half = Q_TILE // 2
    causal = (lax.broadcasted_iota(jnp.int32, (half, KV_TILE), 0) >=
              lax.broadcasted_iota(jnp.int32, (half, KV_TILE), 1))

    def query_heads(qi):
        q = q_ref[0, 0, qi * Q_TILE:(qi + 1) * Q_TILE, :]
        zero = jnp.zeros_like(q)
        return jnp.where(first, q, zero), jnp.where(first, zero, q)

    def head_update(qh, kt, vt, brow, m, l, mask_rows):
        s = lax.dot_general(qh, kt, (((1,), (1,)), ((), ())), preferred_element_type=F32)
        s = s + brow
        if mask_rows is not None:
            top = jnp.where(causal, s[:mask_rows], NEG_BIG)
            s = top if mask_rows == s.shape[0] else jnp.concatenate([top, s[mask_rows:]], axis=0)
        m_new = jnp.maximum(m, jnp.max(s, axis=1, keepdims=True))
        alpha = jnp.exp2(m - m_new)
        p = jnp.exp2(s - m_new)
        l_new = alpha * l + jnp.sum(p, axis=1, keepdims=True)
        pv = jnp.dot(p.astype(BF16), vt, preferred_element_type=F32)
        return m_new, l_new, alpha, pv

    def key_tile(q_heads, t, carry, rows_, mask_rows):
        m0, l0, m1, l1, acc = carry
        kt = k_ref[0, 0, t * KV_TILE:(t + 1) * KV_TILE, :]
        vt = v_ref[0, 0, t * KV_TILE:(t + 1) * KV_TILE, :]
        bt = bias_ref[0, 0, t]
        m0, l0, a0, pv0 = head_update(q_heads[0][rows_], kt, vt, bt[0:1, :], m0, l0, mask_rows)
        m1, l1, a1, pv1 = head_update(q_heads[1][rows_], kt, vt, bt[1:2, :], m1, l1, mask_rows)
        acc = acc * jnp.where(first, a0, a1) + jnp.where(first, pv0, pv1)
        return m0, l0, m1, l1, acc

    everything = slice(0, Q_TILE)
    neg = jnp.full((Q_TILE, 1), NEG_BIG, F32)
    zcol = jnp.zeros((Q_TILE, 1), F32)
    init = (neg, zcol, neg, zcol, jnp.zeros((Q_TILE, PAIR_WIDTH), F32))

    def visible_pair(q_heads, i, carry):
        carry = key_tile(q_heads, 2 * i, carry, everything, None)
        return key_tile(q_heads, 2 * i + 1, carry, everything, None)

    def diagonal_block(q_heads, qi, carry):
        m0, l0, m1, l1, acc = key_tile(q_heads, 2 * qi, carry, everything, half)
        lower = slice(half, Q_TILE)
        low = key_tile(q_heads, 2 * qi + 1,
                       (m0[lower], l0[lower], m1[lower], l1[lower], acc[lower]), lower, half)
        base = qi * Q_TILE
        o_ref[0, base:base + half, :] = (
            acc[:half] * jnp.where(first, 1.0 / l0[:half], 1.0 / l1[:half])).astype(BF16)
        o_ref[0, base + half:base + Q_TILE, :] = (
            low[4] * jnp.where(first, 1.0 / low[1], 1.0 / low[3])).astype(BF16)

    n_q = q_ref.shape[2] // Q_TILE
    phases = []
    for qi in range(n_q):
        for i in range(qi):
            phases.append(("pair", qi, i))
        phases.append(("diag", qi, None))
    assert len(phases) + 1 == SAMPLE_STEPS

    pl.when(first_sidx % groups == 0)(sample_begin)
    sample_step(0)
    q_heads, carry = None, None
    for n, (kind, qi, i) in enumerate(phases):
        if kind == "diag" and qi == 0 or kind == "pair" and i == 0:
            q_heads, carry = query_heads(qi), init
        if kind == "pair":
            carry = visible_pair(q_heads, i, carry)
        else:
            diagonal_block(q_heads, qi, carry)
        sample_step(n + 1)
    pl.when((first_sidx + SAMPLE_STEPS) % groups == 0)(sample_end)

    @pl.when(first_sidx + SAMPLE_STEPS == n_sample_steps)
    def _():
        for u in range(SAMPLE_STEPS - PAGE_SLOTS, SAMPLE_STEPS):
            for c in copies(refill_of(u), u % PAGE_SLOTS):
                c.wait()


def _attention(q, k, v, bias, page_table, qs, k_new, v_new, logf_new,
               cache_kt, cache_vt, cache_lt):
    b, hp, t, w = q.shape
    n_seq, n_new, d = qs.shape
    groups = page_table.shape[1] // PAGES_PER_STEP
    rows = n_new * N_HEADS
    grid = (b, hp)
    n_sample_steps = n_seq * groups
    assert Q_TILE == 2 * KV_TILE and KV_TILE == ROW_TILE and t % Q_TILE == 0
    assert n_sample_steps == b * hp * SAMPLE_STEPS and groups % SAMPLE_STEPS == 0
    assert SAMPLE_STEPS % PAGE_SLOTS == 0

    def seq_of(i, p):
        return ((i * hp + p) * SAMPLE_STEPS) // groups

    new_spec = pl.BlockSpec((1, n_new, d), lambda i, p, pt: (seq_of(i, p), 0, 0))
    any_spec = pl.BlockSpec(memory_space=pl.ANY)
    grid_spec = pltpu.PrefetchScalarGridSpec(
        num_scalar_prefetch=1,
        grid=grid,
        in_specs=[
            pl.BlockSpec((1, 1, t, w), lambda i, p, pt: (i, p, 0, 0)),
            pl.BlockSpec((1, 1, t, w), lambda i, p, pt: (i, p, 0, 0)),
            pl.BlockSpec((1, 1, t, w), lambda i, p, pt: (i, p, 0, 0)),
            pl.BlockSpec((1, 1, t // KV_TILE, 2, KV_TILE), lambda i, p, pt: (i, p, 0, 0, 0)),
            new_spec, new_spec, new_spec,
            pl.BlockSpec((1, n_new, N_HEADS), lambda i, p, pt: (seq_of(i, p), 0, 0)),
            any_spec, any_spec, any_spec,
        ],
        out_specs=[
            pl.BlockSpec((1, t, w), lambda i, p, pt: (i, 0, p)),
            new_spec,
        ],
        scratch_shapes=[
            pltpu.VMEM((PAGE_SLOTS, PAGES_PER_STEP, d, PAGE), F32),
            pltpu.VMEM((PAGE_SLOTS, PAGES_PER_STEP, d, PAGE), F32),
            pltpu.VMEM((PAGE_SLOTS, PAGES_PER_STEP, N_HEADS, PAGE), F32),
            pltpu.SemaphoreType.DMA((3, PAGE_SLOTS)),
            pltpu.VMEM((rows, d), BF16),
            pltpu.VMEM((PAGE, d), F32),
            pltpu.VMEM((PAGE, d), F32),
            pltpu.VMEM((PAGE, N_HEADS), F32),
            pltpu.VMEM((rows, 1), F32),
            pltpu.VMEM((rows, 1), F32),
            pltpu.VMEM((rows, d), F32),
            pltpu.VMEM((N_HEADS, 128), F32),
        ],
    )
    return pl.pallas_call(
        functools.partial(_attn_body, n_new, groups, n_sample_steps),
        grid_spec=grid_spec,
        out_shape=[jax.ShapeDtypeStruct((b, t, hp * w), BF16),
                   jax.ShapeDtypeStruct((n_seq, n_new, d), F32)],
        compiler_params=pltpu.CompilerParams(
            dimension_semantics=("arbitrary", "arbitrary"),
            vmem_limit_bytes=VMEM_LIMIT),
        name="attention",
    )(page_table, q, k, v, bias, qs, k_new, v_new, logf_new, cache_kt, cache_vt, cache_lt)


def _out_body(o_ref, gate_ref, x1_ref, w_ref, g_ref, y_ref):
    og = (o_ref[...].astype(F32) * gate_ref[...].astype(F32)).astype(BF16)
    x2 = x1_ref[...] + jnp.dot(og, w_ref[...], preferred_element_type=F32)
    y_ref[...] = _rms_norm(x2, g_ref[...])


def _out_proj(o, gate, x1, w, g):
    rows, d = x1.shape
    tile = min(OUT_TILE, rows)
    row_spec = pl.BlockSpec((tile, d), lambda i: (i, 0))
    return pl.pallas_call(
        _out_body,
        grid=(rows // tile,),
        in_specs=[row_spec, row_spec, row_spec, _const_spec(w.shape), _const_spec((1, d))],
        out_specs=row_spec,
        out_shape=jax.ShapeDtypeStruct((rows, d), F32),
        compiler_params=pltpu.CompilerParams(
            dimension_semantics=("parallel",), vmem_limit_bytes=VMEM_LIMIT),
        name="out_proj",
    )(o, gate, x1, w, g)


def kernel(x_prompt, x_sample, state_pool, cache_k, cache_v, cache_logf, page_table,
           norm_g, w_in_pool, pool_mix, pool_scale, w_out_pool,
           w_in_attn, b_forget, w_out_attn, final_norm):
    b, t, d = x_prompt.shape
    n_seq, n_new, _ = x_sample.shape
    n_phys = cache_k.shape[1]
    past = page_table.shape[1] * cache_k.shape[2]
    assert norm_g.shape[0] == 2 and d == D_MODEL and cache_k.shape[2] == PAGE

    g0 = norm_g[0].reshape(1, d)
    g1 = norm_g[1].reshape(1, d)
    gf = final_norm.reshape(1, d)
    win = w_in_pool[0].astype(BF16)
    mix = pool_mix[0].astype(BF16)
    scale = pool_scale[0].reshape(1, POOL_WIDTH)
    wout_pool = w_out_pool[0].astype(BF16)
    w_attn = w_in_attn[0, :, :4 * d].astype(BF16)
    wf = jnp.pad(w_in_attn[0, :, 4 * d:], ((0, 0), (0, 128 - N_HEADS))).astype(BF16)
    bf = jnp.pad(b_forget[0], (0, 128 - N_HEADS)).reshape(1, 128)
    wout_attn = w_out_attn[0].astype(BF16)

    x1p, last = _pool_prompt(x_prompt, g0, win, mix, scale, wout_pool)
    xs_tm = x_sample.transpose(1, 0, 2).reshape(n_new * n_seq, d)
    st_tm = state_pool[0].transpose(1, 0, 2).reshape(POOL_CTX * n_seq, POOL_WIDTH)
    x1s_tm, u_tm = _pool_sample(xs_tm, st_tm, n_seq, n_new, past, g0, win, mix, scale, wout_pool)
    x1s = x1s_tm.reshape(n_new, n_seq, d).transpose(1, 0, 2).reshape(n_seq * n_new, d)
    u_new = u_tm.reshape(n_new, n_seq, POOL_WIDTH).transpose(1, 0, 2)

    qp, kp, vp, k32, v32, gate_p, logf_p, bias_p = _proj_prompt(x1p, g1, w_attn, wf, bf)
    qs, ks, vs, gate_s, logf_s = _proj_sample(x1s, g1, w_attn, wf, bf)
    o_p, o_s = _attention(qp, kp, vp, bias_p, page_table,
                          qs.reshape(n_seq, n_new, d), ks.reshape(n_seq, n_new, d),
                          vs.reshape(n_seq, n_new, d), logf_s.reshape(n_seq, n_new, N_HEADS),
                          cache_k[0].transpose(0, 2, 3, 1).reshape(n_phys, d, PAGE),
                          cache_v[0].transpose(0, 2, 3, 1).reshape(n_phys, d, PAGE),
                          cache_logf[0].transpose(0, 2, 1))
    y_p = _out_proj(o_p.reshape(b * t, d), gate_p.reshape(b * t, d), x1p.reshape(b * t, d),
                    wout_attn, gf).reshape(b, t, d)
    y_s = _out_proj(o_s.reshape(n_seq * n_new, d).astype(BF16), gate_s, x1s, wout_attn, gf)

    pool_prompt = last[:, POOL_HALO - POOL_CTX:, :][None]
    pool_sample = jnp.concatenate([state_pool[0][:, n_new:, :], u_new], axis=1)[None]
    heads = (N_HEADS, HEAD_DIM)
    return (y_p, y_s.reshape(n_seq, n_new, d), pool_prompt, pool_sample,
            k32.reshape(1, b, t, *heads), v32.reshape(1, b, t, *heads), logf_p[None],
            ks.reshape(1, n_seq, n_new, *heads), vs.reshape(1, n_seq, n_new, *heads),
            logf_s.reshape(1, n_seq, n_new, N_HEADS))
```

```python
import functools

import jax
import jax.numpy as jnp
from jax import lax
from jax.experimental import pallas as pl
from jax.experimental.pallas import tpu as pltpu

F32 = jnp.float32
BF16 = jnp.bfloat16

D_MODEL = 1024
POOL_WIDTH = 2048
POOL_WINDOWS = (2, 4, 8, 16)
POOL_GROUP = POOL_WIDTH // len(POOL_WINDOWS)
POOL_CTX = max(POOL_WINDOWS) - 1
POOL_HALO = 16
N_HEADS = 16
HEAD_DIM = 64
HEAD_PAIRS = N_HEADS // 2
PAIR_WIDTH = 2 * HEAD_DIM
RMS_EPS = 1e-6
NEG_BIG = -1e30
LOG2E = 1.4426950408889634

POOL_TILE = 1024
ROW_TILE = 512
OUT_TILE = 1024
Q_TILE = 1024
KV_TILE = 512
PAGE = 128
PAGES_PER_STEP = 8
SAMPLE_STEPS = 4
PAGE_SLOTS = 4

VMEM_LIMIT = 56 * 1024 * 1024


def _rms_norm(x, g):
    ms = jnp.mean(x * x, axis=-1, keepdims=True)
    return x * lax.rsqrt(ms + RMS_EPS) * g


def _silu(z):
    return z * jax.nn.sigmoid(z)


def _log_sigmoid(x):
    return jnp.minimum(x, 0.0) - jnp.log1p(jnp.exp(-jnp.abs(x)))


def _const_spec(shape):
    return pl.BlockSpec(shape, lambda *_: (0,) * len(shape), pipeline_mode=pl.Buffered(1))


def _pool_tail(hb, pooled_fn, win_ref, mix_ref, scale_ref, wout_ref, resid):
    acc = resid
    for g in range(len(POOL_WINDOWS)):
        cols = slice(g * POOL_GROUP, (g + 1) * POOL_GROUP)
        zcols = slice(POOL_WIDTH + g * POOL_GROUP, POOL_WIDTH + (g + 1) * POOL_GROUP)
        u = jnp.dot(hb, win_ref[:, cols], preferred_element_type=F32)
        pooled = pooled_fn(g, u)
        y = jnp.dot(pooled.astype(BF16), mix_ref[g], preferred_element_type=F32)
        y = y * scale_ref[:, cols]
        z = jnp.dot(hb, win_ref[:, zcols], preferred_element_type=F32)
        yg = (y * _silu(z)).astype(BF16)
        acc = acc + jnp.dot(yg, wout_ref[cols, :], preferred_element_type=F32)
    return acc


def _pool_prompt_body(x_ref, g_ref, win_ref, mix_ref, scale_ref, wout_ref,
                      x1_ref, last_ref, halo_ref):
    t = pl.program_id(1)

    @pl.when(t == 0)
    def _():
        halo_ref[...] = jnp.zeros_like(halo_ref)

    x = x_ref[0]
    hb = _rms_norm(x, g_ref[...]).astype(BF16)
    pos = t * POOL_TILE + lax.broadcasted_iota(jnp.int32, (POOL_TILE, 1), 0)

    def pooled_fn(g, u):
        w = POOL_WINDOWS[g]
        cols = slice(g * POOL_GROUP, (g + 1) * POOL_GROUP)
        e = jnp.concatenate([halo_ref[:, cols], u], axis=0)
        tail = u[POOL_TILE - POOL_HALO:, :]
        halo_ref[:, cols] = tail
        last_ref[0, :, cols] = tail
        sh = 1
        while sh < w:
            e = e + pltpu.roll(e, sh, 0)
            sh *= 2
        inv = 1.0 / jnp.minimum(pos + 1, w).astype(F32)
        return e[POOL_HALO:, :] * inv - u

    x1_ref[0] = _pool_tail(hb, pooled_fn, win_ref, mix_ref, scale_ref, wout_ref, x)


def _pool_prompt(x, g, win, mix, scale, wout):
    b, t, d = x.shape
    grid = (b, t // POOL_TILE)
    return pl.pallas_call(
        _pool_prompt_body,
        grid=grid,
        in_specs=[
            pl.BlockSpec((1, POOL_TILE, d), lambda i, j: (i, j, 0)),
            _const_spec((1, d)),
            _const_spec(win.shape),
            _const_spec(mix.shape),
            _const_spec((1, POOL_WIDTH)),
            _const_spec(wout.shape),
        ],
        out_specs=[
            pl.BlockSpec((1, POOL_TILE, d), lambda i, j: (i, j, 0)),
            pl.BlockSpec((1, POOL_HALO, POOL_WIDTH), lambda i, j: (i, 0, 0)),
        ],
        out_shape=[
            jax.ShapeDtypeStruct((b, t, d), F32),
            jax.ShapeDtypeStruct((b, POOL_HALO, POOL_WIDTH), F32),
        ],
        scratch_shapes=[pltpu.VMEM((POOL_HALO, POOL_WIDTH), F32)],
        compiler_params=pltpu.CompilerParams(
            dimension_semantics=("parallel", "arbitrary"),
            vmem_limit_bytes=VMEM_LIMIT),
        name="pool_prompt",
    )(x, g, win, mix, scale, wout)


def _pool_sample_body(n_seq, n_new, start, x_ref, st_ref, g_ref, winu_ref, winz_ref, mix_ref,
                      scale_ref, wout_ref, x1_ref, u_ref, hb_ref, pooled_ref):
    grp = pl.program_id(0)

    @pl.when(grp == 0)
    def _():
        x = x_ref[...]
        hb_ref[...] = _rms_norm(x, g_ref[...]).astype(BF16)
        x1_ref[...] = x

    hb = hb_ref[...]
    u = jnp.dot(hb, winu_ref[...], preferred_element_type=F32)
    u_ref[...] = u

    def slab(e):
        if e < POOL_CTX:
            return st_ref[e * n_seq:(e + 1) * n_seq, :]
        return u[(e - POOL_CTX) * n_seq:(e - POOL_CTX + 1) * n_seq, :]

    def pooled(w):
        outs = []
        for t in range(n_new):
            s = slab(POOL_CTX + t)
            for e in range(POOL_CTX + t - w + 1, POOL_CTX + t):
                s = s + slab(e)
            cnt = min(start + t + 1, w)
            outs.append(s * (1.0 / cnt) - slab(POOL_CTX + t))
        return jnp.concatenate(outs, axis=0)

    for g, w in enumerate(POOL_WINDOWS):
        @pl.when(grp == g)
        def _():
            pooled_ref[...] = pooled(w)

    y = jnp.dot(pooled_ref[...].astype(BF16), mix_ref[0], preferred_element_type=F32)
    y = y * scale_ref[...]
    z = jnp.dot(hb, winz_ref[...], preferred_element_type=F32)
    yg = (y * _silu(z)).astype(BF16)
    x1_ref[...] += jnp.dot(yg, wout_ref[...], preferred_element_type=F32)


def _pool_sample(x_tm, state_tm, n_seq, n_new, start, g, win, mix, scale, wout):
    rows, d = x_tm.shape
    n_groups = len(POOL_WINDOWS)
    return pl.pallas_call(
        functools.partial(_pool_sample_body, n_seq, n_new, start),
        grid=(n_groups,),
        in_specs=[
            pl.BlockSpec((rows, d), lambda i: (0, 0)),
            pl.BlockSpec((state_tm.shape[0], POOL_GROUP), lambda i: (0, i)),
            pl.BlockSpec((1, d), lambda i: (0, 0)),
            pl.BlockSpec((d, POOL_GROUP), lambda i: (0, i)),
            pl.BlockSpec((d, POOL_GROUP), lambda i: (0, n_groups + i)),
            pl.BlockSpec((1, POOL_GROUP, POOL_GROUP), lambda i: (i, 0, 0)),
            pl.BlockSpec((1, POOL_GROUP), lambda i: (0, i)),
            pl.BlockSpec((POOL_GROUP, d), lambda i: (i, 0)),
        ],
        out_specs=[
            pl.BlockSpec((rows, d), lambda i: (0, 0)),
            pl.BlockSpec((rows, POOL_GROUP), lambda i: (0, i)),
        ],
        out_shape=[
            jax.ShapeDtypeStruct((rows, d), F32),
            jax.ShapeDtypeStruct((rows, POOL_WIDTH), F32),
        ],
        scratch_shapes=[pltpu.VMEM((rows, d), BF16), pltpu.VMEM((rows, POOL_GROUP), F32)],
        compiler_params=pltpu.CompilerParams(
            dimension_semantics=("arbitrary",), vmem_limit_bytes=VMEM_LIMIT),
        name="pool_sample",
    )(x_tm, state_tm, g, win, win, mix, scale, wout)


def _project(x, g_ref, w_ref, wf_ref, bf_ref):
    hb = _rms_norm(x, g_ref[...]).astype(BF16)
    parts = [jnp.dot(hb, w_ref[:, i * D_MODEL:(i + 1) * D_MODEL], preferred_element_type=F32)
             for i in range(4)]
    f = jnp.dot(hb, wf_ref[...], preferred_element_type=F32) + bf_ref[...]
    q = parts[0] * (LOG2E * HEAD_DIM ** -0.5)
    return q, parts[1], parts[2], _silu(parts[3]), _log_sigmoid(f)


def _lane_cumsum(x):
    lane = lax.broadcasted_iota(jnp.int32, x.shape, 1)
    sh = 1
    while sh < x.shape[1]:
        x = x + jnp.where(lane >= sh, pltpu.roll(x, sh, 1), 0.0)
        sh *= 2
    return x


def _proj_prompt_body(x_ref, g_ref, w_ref, wf_ref, bf_ref,
                      q_ref, kp_ref, vp_ref, k_ref, v_ref, gate_ref, logf_ref, bias_ref,
                      carry_ref):
    t = pl.program_id(1)

    @pl.when(t == 0)
    def _():
        carry_ref[...] = jnp.zeros_like(carry_ref)

    q, k, v, gate, logf = _project(x_ref[0], g_ref, w_ref, wf_ref, bf_ref)
    k_ref[0] = k
    v_ref[0] = v
    gate_ref[0] = gate.astype(BF16)
    logf_ref[0] = logf[:, :N_HEADS]
    qb, kb, vb = q.astype(BF16), k.astype(BF16), v.astype(BF16)
    for p in range(HEAD_PAIRS):
        cols = slice(p * PAIR_WIDTH, (p + 1) * PAIR_WIDTH)
        q_ref[0, p] = qb[:, cols]
        kp_ref[0, p] = kb[:, cols]
        vp_ref[0, p] = vb[:, cols]

    lt = logf.T[:N_HEADS, :]
    carry = carry_ref[...]
    for c in range(ROW_TILE // 128):
        csum = _lane_cumsum(lt[:, c * 128:(c + 1) * 128]) + carry
        neg = csum * (-LOG2E)
        for p in range(HEAD_PAIRS):
            bias_ref[0, p, 0, :, c * 128:(c + 1) * 128] = neg[2 * p:2 * p + 2, :]
        carry = jnp.broadcast_to(csum[:, 127:128], carry.shape)
    carry_ref[...] = carry


def _proj_prompt(x1, g, w, wf, bf):
    b, t, d = x1.shape
    grid = (b, t // ROW_TILE)
    per_kv = KV_TILE // ROW_TILE
    row_spec = pl.BlockSpec((1, ROW_TILE, d), lambda i, j: (i, j, 0))
    pair_spec = pl.BlockSpec((1, HEAD_PAIRS, ROW_TILE, PAIR_WIDTH), lambda i, j: (i, 0, j, 0))
    pair_shape = jax.ShapeDtypeStruct((b, HEAD_PAIRS, t, PAIR_WIDTH), BF16)
    return pl.pallas_call(
        _proj_prompt_body,
        grid=grid,
        in_specs=[row_spec, _const_spec((1, d)), _const_spec(w.shape),
                  _const_spec(wf.shape), _const_spec(bf.shape)],
        out_specs=[
            pair_spec, pair_spec, pair_spec, row_spec, row_spec, row_spec,
            pl.BlockSpec((1, ROW_TILE, N_HEADS), lambda i, j: (i, j, 0)),
            pl.BlockSpec((1, HEAD_PAIRS, 1, 2, ROW_TILE),
                         lambda i, j: (i, 0, j // per_kv, 0, j % per_kv)),
        ],
        out_shape=[
            pair_shape, pair_shape, pair_shape,
            jax.ShapeDtypeStruct((b, t, d), F32),
            jax.ShapeDtypeStruct((b, t, d), F32),
            jax.ShapeDtypeStruct((b, t, d), BF16),
            jax.ShapeDtypeStruct((b, t, N_HEADS), F32),
            jax.ShapeDtypeStruct((b, HEAD_PAIRS, t // KV_TILE, 2, KV_TILE), F32),
        ],
        scratch_shapes=[pltpu.VMEM((N_HEADS, 128), F32)],
        compiler_params=pltpu.CompilerParams(
            dimension_semantics=("parallel", "arbitrary"),
            vmem_limit_bytes=VMEM_LIMIT),
        name="proj_prompt",
    )(x1, g, w, wf, bf)


def _proj_sample_body(x_ref, g_ref, w_ref, wf_ref, bf_ref,
                      q_ref, k_ref, v_ref, gate_ref, logf_ref, hb_ref):
    i = pl.program_id(0)

    @pl.when(i == 0)
    def _():
        hb = _rms_norm(x_ref[...], g_ref[...]).astype(BF16)
        hb_ref[...] = hb
        f = jnp.dot(hb, wf_ref[...], preferred_element_type=F32) + bf_ref[...]
        logf_ref[...] = _log_sigmoid(f)[:, :N_HEADS]

    part = jnp.dot(hb_ref[...], w_ref[...], preferred_element_type=F32)

    @pl.when(i == 0)
    def _():
        q_ref[...] = (part * (LOG2E * HEAD_DIM ** -0.5)).astype(BF16)

    @pl.when(i == 1)
    def _():
        k_ref[...] = part

    @pl.when(i == 2)
    def _():
        v_ref[...] = part

    @pl.when(i == 3)
    def _():
        gate_ref[...] = _silu(part).astype(BF16)


def _proj_sample(x1, g, w, wf, bf):
    rows, d = x1.shape
    full = lambda shape: pl.BlockSpec(shape, lambda i: (0,) * len(shape))
    return pl.pallas_call(
        _proj_sample_body,
        grid=(4,),
        in_specs=[full((rows, d)), full((1, d)), pl.BlockSpec((d, d), lambda i: (0, i)),
                  full(wf.shape), full(bf.shape)],
        out_specs=[full((rows, d)), full((rows, d)), full((rows, d)), full((rows, d)),
                   full((rows, N_HEADS))],
        out_shape=[
            jax.ShapeDtypeStruct((rows, d), BF16),
            jax.ShapeDtypeStruct((rows, d), F32),
            jax.ShapeDtypeStruct((rows, d), F32),
            jax.ShapeDtypeStruct((rows, d), BF16),
            jax.ShapeDtypeStruct((rows, N_HEADS), F32),
        ],
        scratch_shapes=[pltpu.VMEM((rows, d), BF16)],
        compiler_params=pltpu.CompilerParams(
            dimension_semantics=("arbitrary",), vmem_limit_bytes=VMEM_LIMIT),
        name="proj_sample",
    )(x1, g, w, wf, bf)


def _attn_body(n_new, groups, n_sample_steps, pt_ref,
               q_ref, k_ref, v_ref, bias_ref,
               qs_ref, kn_ref, vn_ref, ln_ref, ck_hbm, cv_hbm, cl_hbm,
               o_ref, os_ref,
               kbuf, vbuf, lbuf, sems, qbd_ref, kpad, vpad, lpad,
               m_ref, l_ref, acc_ref, carry_ref):
    flat = pl.program_id(0) * pl.num_programs(1) + pl.program_id(1)
    rows = n_new * N_HEADS

    def copies(sidx, sl):
        bb = sidx // groups
        gg = sidx % groups
        out = []
        for p in range(PAGES_PER_STEP):
            page = pt_ref[bb, (groups - 1 - gg) * PAGES_PER_STEP + p]
            out.append(pltpu.make_async_copy(ck_hbm.at[page], kbuf.at[sl, p], sems.at[0, sl]))
            out.append(pltpu.make_async_copy(cv_hbm.at[page], vbuf.at[sl, p], sems.at[1, sl]))
            out.append(pltpu.make_async_copy(cl_hbm.at[page], lbuf.at[sl, p], sems.at[2, sl]))
        return out

    row_head = lax.broadcasted_iota(jnp.int32, (N_HEADS, D_MODEL), 0)
    lane_head = lax.broadcasted_iota(jnp.int32, (N_HEADS, D_MODEL), 1) // HEAD_DIM
    own = row_head == lane_head

    def suffix_bias(lp_t, carry):
        incl = _lane_cumsum(lp_t)
        total = jnp.broadcast_to(incl[:, 127:128], incl.shape)
        return total - incl + carry, carry + total

    def accumulate(s, pv_fn):
        m_old = m_ref[...]
        m_new = jnp.maximum(m_old, jnp.max(s, axis=1, keepdims=True))
        alpha = jnp.exp2(m_old - m_new)
        p = jnp.exp2(s - m_new)
        l_ref[...] = alpha * l_ref[...] + jnp.sum(p, axis=1, keepdims=True)
        acc_ref[...] = acc_ref[...] * alpha + pv_fn(p.astype(BF16))
        m_ref[...] = m_new

    def sample_begin():
        q = qs_ref[0].astype(F32)
        for t in range(n_new):
            qt = jnp.broadcast_to(q[t:t + 1, :], (N_HEADS, D_MODEL))
            qbd_ref[t * N_HEADS:(t + 1) * N_HEADS, :] = jnp.where(own, qt, 0.0).astype(BF16)
        m_ref[...] = jnp.full_like(m_ref, NEG_BIG)
        l_ref[...] = jnp.zeros_like(l_ref)
        acc_ref[...] = jnp.zeros_like(acc_ref)
        kpad[0:n_new, :] = kn_ref[0]
        vpad[0:n_new, :] = vn_ref[0]
        lpad[0:n_new, :] = ln_ref[0]
        eye = (lax.broadcasted_iota(jnp.int32, (N_HEADS, N_HEADS), 0) ==
               lax.broadcasted_iota(jnp.int32, (N_HEADS, N_HEADS), 1)).astype(F32)
        lp_t = lax.dot_general(eye, lpad[...], (((1,), (1,)), ((), ())),
                               precision=lax.Precision.HIGHEST, preferred_element_type=F32)
        bias, carry = suffix_bias(lp_t * LOG2E, jnp.zeros((N_HEADS, 128), F32))
        carry_ref[...] = carry
        s = lax.dot_general(qbd_ref[...], kpad[...].astype(BF16), (((1,), (1,)), ((), ())),
                            preferred_element_type=F32)
        s = s + jnp.concatenate([bias] * n_new, axis=0)
        q_idx = lax.broadcasted_iota(jnp.int32, (rows, PAGE), 0) // N_HEADS
        k_idx = lax.broadcasted_iota(jnp.int32, (rows, PAGE), 1)
        s = jnp.where(k_idx <= q_idx, s, NEG_BIG)
        vb = vpad[...].astype(BF16)
        accumulate(s, lambda p: jnp.dot(p, vb, preferred_element_type=F32))

    def sample_pages(slot):
        carry = carry_ref[...]
        biases = [None] * PAGES_PER_STEP
        for p in reversed(range(PAGES_PER_STEP)):
            biases[p], carry = suffix_bias(lbuf[slot, p] * LOG2E, carry)
        carry_ref[...] = carry
        bias = jnp.concatenate(biases, axis=1)
        kcat = jnp.concatenate([kbuf[slot, p].astype(BF16) for p in range(PAGES_PER_STEP)], axis=1)
        s = jnp.dot(qbd_ref[...], kcat, preferred_element_type=F32)
        s = s + jnp.concatenate([bias] * n_new, axis=0)
        vcat = jnp.concatenate([vbuf[slot, p].astype(BF16) for p in range(PAGES_PER_STEP)], axis=1)
        accumulate(s, lambda p: lax.dot_general(p, vcat, (((1,), (1,)), ((), ())),
                                                preferred_element_type=F32))

    def sample_end():
        acc = acc_ref[...] * (1.0 / l_ref[...])
        for t in range(n_new):
            blk = jnp.where(own, acc[t * N_HEADS:(t + 1) * N_HEADS, :], 0.0)
            os_ref[0, t:t + 1, :] = jnp.sum(blk, axis=0, keepdims=True)

    @pl.when(flat == 0)
    def _():
        for u in range(PAGE_SLOTS):
            for c in copies(u, u):
                c.start()
        kpad[...] = jnp.zeros_like(kpad)
        vpad[...] = jnp.zeros_like(vpad)
        lpad[...] = jnp.zeros_like(lpad)

    first_sidx = flat * SAMPLE_STEPS

    def refill_of(u):
        return (first_sidx + u + PAGE_SLOTS) % n_sample_steps

    def sample_step(u):
        slot = u % PAGE_SLOTS
        for c in copies(first_sidx + u, slot):
            c.wait()
        sample_pages(slot)
        for c in copies(refill_of(u), slot):
            c.start()

    lane = lax.broadcasted_iota(jnp.int32, (1, PAIR_WIDTH), 1)
    first = lane < HEAD_DIM
    half = Q_TILE // 2
    causal = (lax.broadcasted_iota(jnp.int32, (half, KV_TILE), 0) >=
              lax.broadcasted_iota(jnp.int32, (half, KV_TILE), 1))

    def query_heads(qi):
        q = q_ref[0, 0, qi * Q_TILE:(qi + 1) * Q_TILE, :]
        zero = jnp.zeros_like(q)
        return jnp.where(first, q, zero), jnp.where(first, zero, q)

    def head_update(qh, kt, vt, brow, m, l, mask_rows):
        s = lax.dot_general(qh, kt, (((1,), (1,)), ((), ())), preferred_element_type=F32)
        s = s + brow
        if mask_rows is not None:
            top = jnp.where(causal, s[:mask_rows], NEG_BIG)
            s = top if mask_rows == s.shape[0] else jnp.concatenate([top, s[mask_rows:]], axis=0)
        m_new = jnp.maximum(m, jnp.max(s, axis=1, keepdims=True))
        alpha = jnp.exp2(m - m_new)
        p = jnp.exp2(s - m_new)
        l_new = alpha * l + jnp.sum(p, axis=1, keepdims=True)
        pv = jnp.dot(p.astype(BF16), vt, preferred_element_type=F32)
        return m_new, l_new, alpha, pv

    def key_tile(q_heads, t, carry, rows_, mask_rows):
        m0, l0, m1, l1, acc = carry
        kt = k_ref[0, 0, t * KV_TILE:(t + 1) * KV_TILE, :]
        vt = v_ref[0, 0, t * KV_TILE:(t + 1) * KV_TILE, :]
        bt = bias_ref[0, 0, t]
        m0, l0, a0, pv0 = head_update(q_heads[0][rows_], kt, vt, bt[0:1, :], m0, l0, mask_rows)
        m1, l1, a1, pv1 = head_update(q_heads[1][rows_], kt, vt, bt[1:2, :], m1, l1, mask_rows)
        acc = acc * jnp.where(first, a0, a1) + jnp.where(first, pv0, pv1)
        return m0, l0, m1, l1, acc

    everything = slice(0, Q_TILE)
    neg = jnp.full((Q_TILE, 1), NEG_BIG, F32)
    zcol = jnp.zeros((Q_TILE, 1), F32)
    init = (neg, zcol, neg, zcol, jnp.zeros((Q_TILE, PAIR_WIDTH), F32))

    def visible_pair(q_heads, i, carry):
        carry = key_tile(q_heads, 2 * i, carry, everything, None)
        return key_tile(q_heads, 2 * i + 1, carry, everything, None)

    def diagonal_block(q_heads, qi, carry):
        m0, l0, m1, l1, acc = key_tile(q_heads, 2 * qi, carry, everything, half)
        lower = slice(half, Q_TILE)
        low = key_tile(q_heads, 2 * qi + 1,
                       (m0[lower], l0[lower], m1[lower], l1[lower], acc[lower]), lower, half)
        base = qi * Q_TILE
        o_ref[0, base:base + half, :] = (
            acc[:half] * jnp.where(first, 1.0 / l0[:half], 1.0 / l1[:half])).astype(BF16)
        o_ref[0, base + half:base + Q_TILE, :] = (
            low[4] * jnp.where(first, 1.0 / low[1], 1.0 / low[3])).astype(BF16)

    n_q = q_ref.shape[2] // Q_TILE
    phases = []
    for qi in range(n_q):
        for i in range(qi):
            phases.append(("pair", qi, i))
        phases.append(("diag", qi, None))
    assert len(phases) + 1 == SAMPLE_STEPS

    pl.when(first_sidx % groups == 0)(sample_begin)
    sample_step(0)
    q_heads, carry = None, None
    for n, (kind, qi, i) in enumerate(phases):
        if kind == "diag" and qi == 0 or kind == "pair" and i == 0:
            q_heads, carry = query_heads(qi), init
        if kind == "pair":
            carry = visible_pair(q_heads, i, carry)
        else:
            diagonal_block(q_heads, qi, carry)
        sample_step(n + 1)
    pl.when((first_sidx + SAMPLE_STEPS) % groups == 0)(sample_end)

    @pl.when(first_sidx + SAMPLE_STEPS == n_sample_steps)
    def _():
        for u in range(SAMPLE_STEPS - PAGE_SLOTS, SAMPLE_STEPS):
            for c in copies(refill_of(u), u % PAGE_SLOTS):
                c.wait()


def _attention(q, k, v, bias, page_table, qs, k_new, v_new, logf_new,
               cache_kt, cache_vt, cache_lt):
    b, hp, t, w = q.shape
    n_seq, n_new, d = qs.shape
    groups = page_table.shape[1] // PAGES_PER_STEP
    rows = n_new * N_HEADS
    grid = (b, hp)
    n_sample_steps = n_seq * groups
    assert Q_TILE == 2 * KV_TILE and KV_TILE == ROW_TILE and t % Q_TILE == 0
    assert n_sample_steps == b * hp * SAMPLE_STEPS and groups % SAMPLE_STEPS == 0
    assert SAMPLE_STEPS % PAGE_SLOTS == 0

    def seq_of(i, p):
        return ((i * hp + p) * SAMPLE_STEPS) // groups

    new_spec = pl.BlockSpec((1, n_new, d), lambda i, p, pt: (seq_of(i, p), 0, 0))
    any_spec = pl.BlockSpec(memory_space=pl.ANY)
    grid_spec = pltpu.PrefetchScalarGridSpec(
        num_scalar_prefetch=1,
        grid=grid,
        in_specs=[
            pl.BlockSpec((1, 1, t, w), lambda i, p, pt: (i, p, 0, 0)),
            pl.BlockSpec((1, 1, t, w), lambda i, p, pt: (i, p, 0, 0)),
            pl.BlockSpec((1, 1, t, w), lambda i, p, pt: (i, p, 0, 0)),
            pl.BlockSpec((1, 1, t // KV_TILE, 2, KV_TILE), lambda i, p, pt: (i, p, 0, 0, 0)),
            new_spec, new_spec, new_spec,
            pl.BlockSpec((1, n_new, N_HEADS), lambda i, p, pt: (seq_of(i, p), 0, 0)),
            any_spec, any_spec, any_spec,
        ],
        out_specs=[
            pl.BlockSpec((1, t, w), lambda i, p, pt: (i, 0, p)),
            new_spec,
        ],
        scratch_shapes=[
            pltpu.VMEM((PAGE_SLOTS, PAGES_PER_STEP, d, PAGE), F32),
            pltpu.VMEM((PAGE_SLOTS, PAGES_PER_STEP, d, PAGE), F32),
            pltpu.VMEM((PAGE_SLOTS, PAGES_PER_STEP, N_HEADS, PAGE), F32),
            pltpu.SemaphoreType.DMA((3, PAGE_SLOTS)),
            pltpu.VMEM((rows, d), BF16),
            pltpu.VMEM((PAGE, d), F32),
            pltpu.VMEM((PAGE, d), F32),
            pltpu.VMEM((PAGE, N_HEADS), F32),
            pltpu.VMEM((rows, 1), F32),
            pltpu.VMEM((rows, 1), F32),
            pltpu.VMEM((rows, d), F32),
            pltpu.VMEM((N_HEADS, 128), F32),
        ],
    )
    return pl.pallas_call(
        functools.partial(_attn_body, n_new, groups, n_sample_steps),
        grid_spec=grid_spec,
        out_shape=[jax.ShapeDtypeStruct((b, t, hp * w), BF16),
                   jax.ShapeDtypeStruct((n_seq, n_new, d), F32)],
        compiler_params=pltpu.CompilerParams(
            dimension_semantics=("arbitrary", "arbitrary"),
            vmem_limit_bytes=VMEM_LIMIT),
        name="attention",
    )(page_table, q, k, v, bias, qs, k_new, v_new, logf_new, cache_kt, cache_vt, cache_lt)


def _out_body(o_ref, gate_ref, x1_ref, w_ref, g_ref, y_ref):
    og = (o_ref[...].astype(F32) * gate_ref[...].astype(F32)).astype(BF16)
    x2 = x1_ref[...] + jnp.dot(og, w_ref[...], preferred_element_type=F32)
    y_ref[...] = _rms_norm(x2, g_ref[...])


def _out_proj(o, gate, x1, w, g):
    rows, d = x1.shape
    tile = min(OUT_TILE, rows)
    row_spec = pl.BlockSpec((tile, d), lambda i: (i, 0))
    return pl.pallas_call(
        _out_body,
        grid=(rows // tile,),
        in_specs=[row_spec, row_spec, row_spec, _const_spec(w.shape), _const_spec((1, d))],
        out_specs=row_spec,
        out_shape=jax.ShapeDtypeStruct((rows, d), F32),
        compiler_params=pltpu.CompilerParams(
            dimension_semantics=("parallel",), vmem_limit_bytes=VMEM_LIMIT),
        name="out_proj",
    )(o, gate, x1, w, g)


def kernel(x_prompt, x_sample, state_pool, cache_k, cache_v, cache_logf, page_table,
           norm_g, w_in_pool, pool_mix, pool_scale, w_out_pool,
           w_in_attn, b_forget, w_out_attn, final_norm):
    b, t, d = x_prompt.shape
    n_seq, n_new, _ = x_sample.shape
    n_phys = cache_k.shape[1]
    past = page_table.shape[1] * cache_k.shape[2]
    assert norm_g.shape[0] == 2 and d == D_MODEL and cache_k.shape[2] == PAGE

    g0 = norm_g[0].reshape(1, d)
    g1 = norm_g[1].reshape(1, d)
    gf = final_norm.reshape(1, d)
    win = w_in_pool[0].astype(BF16)
    mix = pool_mix[0].astype(BF16)
    scale = pool_scale[0].reshape(1, POOL_WIDTH)
    wout_pool = w_out_pool[0].astype(BF16)
    w_attn = w_in_attn[0, :, :4 * d].astype(BF16)
    wf = jnp.pad(w_in_attn[0, :, 4 * d:], ((0, 0), (0, 128 - N_HEADS))).astype(BF16)
    bf = jnp.pad(b_forget[0], (0, 128 - N_HEADS)).reshape(1, 128)
    wout_attn = w_out_attn[0].astype(BF16)

    x1p, last = _pool_prompt(x_prompt, g0, win, mix, scale, wout_pool)
    xs_tm = x_sample.transpose(1, 0, 2).reshape(n_new * n_seq, d)
    st_tm = state_pool[0].transpose(1, 0, 2).reshape(POOL_CTX * n_seq, POOL_WIDTH)
    x1s_tm, u_tm = _pool_sample(xs_tm, st_tm, n_seq, n_new, past, g0, win, mix, scale, wout_pool)
    x1s = x1s_tm.reshape(n_new, n_seq, d).transpose(1, 0, 2).reshape(n_seq * n_new, d)
    u_new = u_tm.reshape(n_new, n_seq, POOL_WIDTH).transpose(1, 0, 2)

    qp, kp, vp, k32, v32, gate_p, logf_p, bias_p = _proj_prompt(x1p, g1, w_attn, wf, bf)
    qs, ks, vs, gate_s, logf_s = _proj_sample(x1s, g1, w_attn, wf, bf)
    o_p, o_s = _attention(qp, kp, vp, bias_p, page_table,
                          qs.reshape(n_seq, n_new, d), ks.reshape(n_seq, n_new, d),
                          vs.reshape(n_seq, n_new, d), logf_s.reshape(n_seq, n_new, N_HEADS),
                          cache_k[0].transpose(0, 2, 3, 1).reshape(n_phys, d, PAGE),
                          cache_v[0].transpose(0, 2, 3, 1).reshape(n_phys, d, PAGE),
                          cache_logf[0].transpose(0, 2, 1))
    y_p = _out_proj(o_p.reshape(b * t, d), gate_p.reshape(b * t, d), x1p.reshape(b * t, d),
                    wout_attn, gf).reshape(b, t, d)
    y_s = _out_proj(o_s.reshape(n_seq * n_new, d).astype(BF16), gate_s, x1s, wout_attn, gf)

    pool_prompt = last[:, POOL_HALO - POOL_CTX:, :][None]
    pool_sample = jnp.concatenate([state_pool[0][:, n_new:, :], u_new], axis=1)[None]
    heads = (N_HEADS, HEAD_DIM)
    return (y_p, y_s.reshape(n_seq, n_new, d), pool_prompt, pool_sample,
            k32.reshape(1, b, t, *heads), v32.reshape(1, b, t, *heads), logf_p[None],
            ks.reshape(1, n_seq, n_new, *heads), vs.reshape(1, n_seq, n_new, *heads),
            logf_s.reshape(1, n_seq, n_new, N_HEADS))
```
